```python
import math
import jax, jax.numpy as jnp
from jax import lax
import numpy as np


D_MODEL = 1024
BATCH = 8
SEQ = 8192
DEPTH = 2

CHUNK = 64
Q_BLOCK = 128
HEAD_DIM = 64
N_GROUP_HEADS = 4
GROUP_WIDTH = N_GROUP_HEADS * HEAD_DIM
N_GROUPS = 4
D_MIX = N_GROUPS * GROUP_WIDTH
ROPE_THETA = 500000.0
RET_THETA = 10000.0
KV_RANK = 128
IDX_HEADS = 8
IDX_DIM = 64
TOPK_MAX = 256
DIFF_HALF = HEAD_DIM // 2
EPS = 1e-6

IN_SIZES = (
    GROUP_WIDTH, KV_RANK, IDX_HEADS * IDX_DIM, IDX_DIM, IDX_HEADS, GROUP_WIDTH,
    GROUP_WIDTH, GROUP_WIDTH, GROUP_WIDTH, GROUP_WIDTH,
    GROUP_WIDTH, GROUP_WIDTH, GROUP_WIDTH, GROUP_WIDTH,
    GROUP_WIDTH, GROUP_WIDTH, GROUP_WIDTH, GROUP_WIDTH,
)
D_IN = 14 * GROUP_WIDTH + KV_RANK + IDX_HEADS * IDX_DIM + IDX_DIM + IDX_HEADS

kernel_name = 'hybrid_head_group_streaming_encoder'


def _rms(x, g=None):
    xf = x.astype(jnp.float32)
    y = xf * lax.rsqrt(jnp.mean(xf * xf, axis=-1, keepdims=True) + EPS)
    if g is not None:
        y = y * g.astype(jnp.float32)
    return y.astype(x.dtype)


def _rope(x, pos, theta, rot_dim):
    half = rot_dim // 2
    inv = theta ** (-jnp.arange(half, dtype=jnp.float32) * 2.0 / rot_dim)
    ang = pos.astype(jnp.float32)[..., None] * inv
    cos = jnp.cos(ang)[:, :, None, :]
    sin = jnp.sin(ang)[:, :, None, :]
    xr = x[..., :rot_dim].astype(jnp.float32)
    x1, x2 = xr[..., :half], xr[..., half:]
    rot = jnp.concatenate([x1 * cos - x2 * sin, x2 * cos + x1 * sin], axis=-1)
    return jnp.concatenate([rot.astype(x.dtype), x[..., rot_dim:]], axis=-1)


def _split(t, sizes):
    return jnp.split(t, np.cumsum(sizes)[:-1].tolist(), axis=-1)


def _qslice(t, i):
    return lax.dynamic_slice_in_dim(t, i * Q_BLOCK, Q_BLOCK, axis=1)


def _sweep(block_fn, S):
    out = lax.map(block_fn, jnp.arange(S // Q_BLOCK))
    nb, B, qb, H, e = out.shape
    return jnp.moveaxis(out, 0, 1).reshape(B, nb * qb, H, e)


def _chunk_mask(tq, S):
    end = (tq // CHUNK + 1) * CHUNK
    return jnp.arange(S)[None, :] < end[:, None], end


def _dsa(q, k, v, q_idx, k_idx, w_idx):
    S = q.shape[1]
    topk = min(TOPK_MAX, S // 4)
    ts = jnp.arange(S)
    idx_scale = (IDX_HEADS * IDX_DIM) ** -0.5

    def block(i):
        tq = i * Q_BLOCK + jnp.arange(Q_BLOCK)
        _, end = _chunk_mask(tq, S)
        rel = jax.nn.relu(jnp.einsum('bqhd,bsd->bqhs', _qslice(q_idx, i), k_idx).astype(jnp.float32))
        score = jnp.einsum('bqhs,bqh->bqs', rel, _qslice(w_idx, i).astype(jnp.float32)) * idx_scale
        score = jnp.where(ts[None, None, :] < end[None, :, None], score, -jnp.inf)
        _, sel = lax.top_k(score, topk)
        valid = sel < end[None, :, None]
        k_sel = jax.vmap(lambda kb, ib: kb[ib])(k, sel)
        v_sel = jax.vmap(lambda vb, ib: vb[ib])(v, sel)
        logits = jnp.einsum('bqhd,bqkd->bqhk', _qslice(q, i), k_sel).astype(jnp.float32) * HEAD_DIM ** -0.5
        logits = jnp.where(valid[:, :, None, :], logits, -jnp.inf)
        p = jax.nn.softmax(logits, axis=-1)
        return jnp.einsum('bqhk,bqkd->bqhd', p.astype(v.dtype), v_sel)

    return _sweep(block, S)


def _retention(q, k, v, g):
    B, S, H, d = q.shape
    e = v.shape[-1]
    nc = S // CHUNK
    log_gamma = jnp.log1p(-(2.0 ** (-5.0 - jnp.arange(H, dtype=jnp.float32))))
    pos_c = jnp.arange(CHUNK, dtype=jnp.float32)
    intra_decay = jnp.exp(log_gamma[:, None, None] * jnp.abs(pos_c[:, None] - pos_c[None, :]))
    k_decay = jnp.exp(log_gamma[:, None] * (CHUNK - 1 - pos_c))
    q_decay = jnp.exp(log_gamma[:, None] * (pos_c + 1.0))
    chunk_decay = jnp.exp(log_gamma * CHUNK)
    qc = q.astype(jnp.float32).reshape(B, nc, CHUNK, H, d)
    kc = k.astype(jnp.float32).reshape(B, nc, CHUNK, H, d) * d ** -0.5
    vc = v.astype(jnp.float32).reshape(B, nc, CHUNK, H, e)
    s_intra = jnp.einsum('bnihd,bnjhd->bnhij', qc, kc) * intra_decay
    y = jnp.einsum('bnhij,bnjhe->bnihe', s_intra, vc)
    kv = jnp.einsum('bnjhd,bnjhe,hj->nbhde', kc, vc, k_decay)

    def step(state, kv_n):
        return chunk_decay[None, :, None, None] * state + kv_n, state

    _, states = lax.scan(step, jnp.zeros((B, H, d, e), jnp.float32), kv)
    y = y + jnp.einsum('bnihd,nbhde,hi->bnihe', qc, states, q_decay)
    y = y.reshape(B, S, H, e)
    mu = jnp.mean(y, axis=-1, keepdims=True)
    var = jnp.mean(jnp.square(y - mu), axis=-1, keepdims=True)
    y = ((y - mu) * lax.rsqrt(var + EPS)).reshape(B, S, H * e) * g.astype(jnp.float32)
    return y.astype(v.dtype)


def _diff_attn(q1, q2, k1, k2, v, lam):
    S = q1.shape[1]
    scale = DIFF_HALF ** -0.5

    def block(i):
        tq = i * Q_BLOCK + jnp.arange(Q_BLOCK)
        mask, _ = _chunk_mask(tq, S)

        def probs(qh, kh):
            s = jnp.einsum('bqhd,bshd->bhqs', _qslice(qh, i), kh).astype(jnp.float32) * scale
            return jax.nn.softmax(jnp.where(mask, s, -jnp.inf), axis=-1)

        a = probs(q1, k1) - lam * probs(q2, k2)
        return jnp.einsum('bhqs,bshe->bqhe', a.astype(v.dtype), v)

    return _sweep(block, S)


def _stick_breaking(q, k, v):
    S = q.shape[1]
    ts = jnp.arange(S)

    def block(i):
        tq = i * Q_BLOCK + jnp.arange(Q_BLOCK)
        strict = ts[None, :] < tq[:, None]
        z = jnp.einsum('bqhd,bshd->bhqs', _qslice(q, i), k).astype(jnp.float32) * HEAD_DIM ** -0.5
        log_1m = jnp.where(strict, jax.nn.log_sigmoid(-z), 0.0)
        log_rest = lax.cumsum(log_1m, axis=3, reverse=True) - log_1m
        a = jnp.where(strict, jnp.exp(jax.nn.log_sigmoid(z) + log_rest), 0.0)
        return jnp.einsum('bhqs,bshe->bqhe', a.astype(v.dtype), v)

    return _sweep(block, S)


def setup_inputs(seed: int = 0) -> dict:
    key = jax.random.key(seed)
    ks = jax.random.split(key, 20)
    f32 = jnp.float32

    def nrm(k, shape, scale):
        return jax.random.normal(k, shape, f32) * scale

    def gain(k, shape):
        return 1.0 + 0.02 * jax.random.normal(k, shape, f32)

    x = jax.random.normal(ks[0], (BATCH, SEQ, D_MODEL), f32)
    offs = jax.random.randint(ks[1], (BATCH, 1), 0, 4096, dtype=jnp.int32)
    positions = (jnp.arange(SEQ, dtype=jnp.int32)[None, :] + offs).astype(jnp.int32)
    return {
        'x': x,
        'positions': positions,
        'norm_g': gain(ks[2], (DEPTH, D_MODEL)),
        'w_in': nrm(ks[3], (DEPTH, D_MODEL, D_IN), D_MODEL ** -0.5),
        'kv_norm_g': gain(ks[4], (DEPTH, KV_RANK)),
        'w_kv_up': nrm(ks[5], (DEPTH, KV_RANK, 2 * HEAD_DIM), KV_RANK ** -0.5),
        'q_norm_a': gain(ks[6], (DEPTH, HEAD_DIM)),
        'k_norm_a': gain(ks[7], (DEPTH, HEAD_DIM)),
        'ret_norm_g': gain(ks[8], (DEPTH, GROUP_WIDTH)),
        'q_norm_c': gain(ks[9], (DEPTH, DIFF_HALF)),
        'k_norm_c': gain(ks[10], (DEPTH, DIFF_HALF)),
        'lam_q1': nrm(ks[11], (DEPTH, DIFF_HALF), 0.1),
        'lam_k1': nrm(ks[12], (DEPTH, DIFF_HALF), 0.1),
        'lam_q2': nrm(ks[13], (DEPTH, DIFF_HALF), 0.1),
        'lam_k2': nrm(ks[14], (DEPTH, DIFF_HALF), 0.1),
        'subln_g': gain(ks[15], (DEPTH, HEAD_DIM)),
        'w_out': nrm(ks[16], (DEPTH, D_MIX, D_MODEL), D_MIX ** -0.5),
    }


def reference(x, positions, norm_g, w_in, kv_norm_g, w_kv_up, q_norm_a, k_norm_a, ret_norm_g,
              q_norm_c, k_norm_c, lam_q1, lam_k1, lam_q2, lam_k2, subln_g, w_out):
    B, S, _ = x.shape

    def heads(t, d):
        return t.reshape(B, S, -1, d)

    for l in range(DEPTH):
        h = _rms(x, norm_g[l])
        proj = jnp.einsum('bsd,df->bsf', h, w_in[l])
        (qa, ckv, qidx, kidx, widx, ga,
         qb, kb, vb, gb,
         qc, kc, vc, gc,
         qd, kd, vd, gd) = _split(proj, IN_SIZES)

        kv_a = jnp.einsum('bsr,rf->bsf', _rms(ckv, kv_norm_g[l]), w_kv_up[l])
        k_a, v_a = kv_a[..., :HEAD_DIM], kv_a[..., HEAD_DIM:]
        q_a = _rope(_rms(heads(qa, HEAD_DIM), q_norm_a[l]), positions, ROPE_THETA, HEAD_DIM // 4)
        k_a = _rope(_rms(k_a, k_norm_a[l])[:, :, None, :], positions, ROPE_THETA, HEAD_DIM // 4)[:, :, 0, :]
        q_i = _rope(heads(qidx, IDX_DIM), positions, ROPE_THETA, IDX_DIM // 4)
        k_i = _rope(_rms(kidx)[:, :, None, :], positions, ROPE_THETA, IDX_DIM // 4)[:, :, 0, :]
        y_a = _dsa(q_a, k_a, v_a, q_i, k_i, widx).reshape(B, S, GROUP_WIDTH)

        q_b = _rope(heads(qb, HEAD_DIM), positions, RET_THETA, HEAD_DIM)
        k_b = _rope(heads(kb, HEAD_DIM), positions, RET_THETA, HEAD_DIM)
        y_b = _retention(q_b, k_b, heads(vb, HEAD_DIM), ret_norm_g[l])

        qc4 = heads(qc, 2 * DIFF_HALF)
        kc4 = heads(kc, 2 * DIFF_HALF)
        prep_q = lambda t: _rope(_rms(t, q_norm_c[l]), positions, ROPE_THETA, DIFF_HALF // 4)
        prep_k = lambda t: _rope(_rms(t, k_norm_c[l]), positions, ROPE_THETA, DIFF_HALF // 4)
        q1, q2 = prep_q(qc4[..., :DIFF_HALF]), prep_q(qc4[..., DIFF_HALF:])
        k1, k2 = prep_k(kc4[..., :DIFF_HALF]), prep_k(kc4[..., DIFF_HALF:])
        lam_init = 0.8 - 0.6 * math.exp(-0.3 * l)
        lam = (jnp.exp(jnp.sum(lam_q1[l].astype(jnp.float32) * lam_k1[l].astype(jnp.float32)))
               - jnp.exp(jnp.sum(lam_q2[l].astype(jnp.float32) * lam_k2[l].astype(jnp.float32)))
               + lam_init)
        y_c = _diff_attn(q1, q2, k1, k2, heads(vc, HEAD_DIM), lam)
        y_c = (_rms(y_c, subln_g[l]) * (1.0 - lam_init)).reshape(B, S, GROUP_WIDTH)

        y_d = _stick_breaking(heads(qd, HEAD_DIM), heads(kd, HEAD_DIM), heads(vd, HEAD_DIM)).reshape(B, S, GROUP_WIDTH)

        y = jnp.concatenate([
            y_a * jax.nn.silu(ga),
            y_b * jax.nn.silu(gb),
            y_c * jax.nn.silu(gc),
            y_d * jax.nn.silu(gd),
        ], axis=-1)
        x = x + jnp.einsum('bsf,fd->bsd', y, w_out[l])
    return x
```

```python
import functools
import math

import numpy as np
import jax
import jax.numpy as jnp
from jax import lax
from jax.experimental import pallas as pl
from jax.experimental.pallas import tpu as pltpu

F32 = jnp.float32
BF16 = jnp.bfloat16
I32 = jnp.int32

LANES = 128
CHUNK = 64
HEAD_DIM = 64
N_GROUP_HEADS = 4
GROUP_WIDTH = N_GROUP_HEADS * HEAD_DIM
KV_RANK = 128
IDX_HEADS = 8
IDX_DIM = 64
TOPK_MAX = 256
DIFF_HALF = HEAD_DIM // 2
ROPE_THETA = 500000.0
RET_THETA = 10000.0
EPS = 1e-6

ROW_TILE = 256
SEQ_TILE = 256
VMEM_LIMIT = 56 * 1024 * 1024
INT_MIN = -2 ** 31
NEG_BIG = -1e30

QA_W = N_GROUP_HEADS * LANES
C_QA = 0
C_CKV = C_QA + QA_W
C_QIDX = C_CKV + KV_RANK
C_KIDX = C_QIDX + IDX_HEADS * IDX_DIM
C_WIDX = C_KIDX + LANES
C_B = C_WIDX + LANES
C_C = C_B + 3 * GROUP_WIDTH
C_D = C_C + 3 * GROUP_WIDTH
C_GATE = C_D + 3 * GROUP_WIDTH
PROJ_W = C_GATE + 4 * GROUP_WIDTH


def _dot(a, b):
    return jnp.dot(a, b, preferred_element_type=F32)


def _dot_nt(a, b):
    return lax.dot_general(a, b, (((1,), (1,)), ((), ())), preferred_element_type=F32)


def _dot_tn(a, b):
    return lax.dot_general(a, b, (((0,), (0,)), ((), ())), preferred_element_type=F32)


def _lane_iota(shape):
    return lax.broadcasted_iota(I32, shape, len(shape) - 1)


def _group_of(idx, size):
    return jnp.right_shift(idx, int(math.log2(size)))


def _rope(x, cos, sin, half, first_mask):
    partner = jnp.where(first_mask, pltpu.roll(x, LANES - half, 1), pltpu.roll(x, half, 1))
    return x * cos + partner * sin


def _proj_kernel(x_ref, ng_ref, w_ref, wkv_ref, gains_ref, c1_ref, s1_ref, c2_ref, s2_ref,
                 c3_ref, s3_ref,
                 qa_o, kv_o, qidx_o, kidx_o, widx_o, qb_o, kb_o, vb_o, qc_o, kc_o, vc_o,
                 qd_o, kd_o, vd_o, gate_o):
    tm = x_ref.shape[0]
    x = x_ref[...]
    r = lax.rsqrt(jnp.mean(x * x, axis=-1, keepdims=True) + EPS)
    h = (x * r * ng_ref[...]).astype(BF16)

    def proj(lo, width):
        return _dot(h, w_ref[:, lo:lo + width])

    lane = _lane_iota((tm, LANES))
    low_half = lane < HEAD_DIM
    first1 = (lane & (HEAD_DIM - 1)) < HEAD_DIM // 8
    first2 = (lane & (HEAD_DIM - 1)) < HEAD_DIM // 2
    first3 = (lane & (DIFF_HALF - 1)) < DIFF_HALF // 8
    c1, s1 = c1_ref[...], s1_ref[...]
    c2, s2 = c2_ref[...], s2_ref[...]
    c3, s3 = c3_ref[...], s3_ref[...]
    g_qa = gains_ref[0:1, :]
    g_kv = gains_ref[1:2, :]
    g_ka = gains_ref[2:3, :]
    g_qc = gains_ref[3:4, :]
    g_kc = gains_ref[4:5, :]

    p = proj(C_QA, QA_W)
    for hd in range(N_GROUP_HEADS):
        blk = p[:, hd * LANES:(hd + 1) * LANES]
        rr = lax.rsqrt(jnp.sum(blk * blk, axis=-1, keepdims=True) * (1.0 / HEAD_DIM) + EPS)
        y = _rope(blk * rr * g_qa, c1, s1, HEAD_DIM // 8, first1)
        qa_o[:, hd * LANES:(hd + 1) * LANES] = (y * HEAD_DIM ** -0.5).astype(BF16)

    p = proj(C_CKV, KV_RANK)
    rr = lax.rsqrt(jnp.mean(p * p, axis=-1, keepdims=True) + EPS)
    ckv = (p * rr * g_kv).astype(BF16)
    kv = _dot(ckv, wkv_ref[...])
    ssq = jnp.sum(jnp.where(low_half, kv * kv, 0.0), axis=-1, keepdims=True)
    rk = lax.rsqrt(ssq * (1.0 / HEAD_DIM) + EPS)
    kr = _rope(kv * rk * g_ka, c1, s1, HEAD_DIM // 8, first1)
    kv_o[...] = jnp.where(low_half, kr, kv).astype(BF16)

    p = proj(C_QIDX, IDX_HEADS * IDX_DIM)
    for blk_i in range(IDX_HEADS * IDX_DIM // LANES):
        blk = p[:, blk_i * LANES:(blk_i + 1) * LANES]
        qidx_o[:, blk_i * LANES:(blk_i + 1) * LANES] = _rope(
            blk, c1, s1, IDX_DIM // 8, first1).astype(BF16)
    p = proj(C_KIDX, LANES)
    rr = lax.rsqrt(jnp.sum(p * p, axis=-1, keepdims=True) * (0.5 / IDX_DIM) + EPS)
    kidx_o[...] = _rope(p * rr, c1, s1, IDX_DIM // 8, first1).astype(BF16)
    widx_o[...] = proj(C_WIDX, LANES)

    p = proj(C_B, 3 * GROUP_WIDTH)
    for blk_i in range(GROUP_WIDTH // LANES):
        sl = slice(blk_i * LANES, (blk_i + 1) * LANES)
        qb_o[:, sl] = _rope(p[:, sl], c2, s2, HEAD_DIM // 2, first2).astype(BF16)
        kblk = p[:, GROUP_WIDTH + blk_i * LANES:GROUP_WIDTH + (blk_i + 1) * LANES]
        kb_o[:, sl] = (_rope(kblk, c2, s2, HEAD_DIM // 2, first2) * HEAD_DIM ** -0.5).astype(BF16)
    vb_o[...] = p[:, 2 * GROUP_WIDTH:].astype(BF16)

    row = lax.broadcasted_iota(I32, (LANES, LANES), 0)
    col = lax.broadcasted_iota(I32, (LANES, LANES), 1)
    group_ones = jnp.where(_group_of(row, DIFF_HALF) == _group_of(col, DIFF_HALF),
                           1.0, 0.0).astype(BF16)
    p = proj(C_C, 3 * GROUP_WIDTH)
    for seg, (gain, out) in enumerate(((g_qc, qc_o), (g_kc, kc_o))):
        for blk_i in range(GROUP_WIDTH // LANES):
            lo = seg * GROUP_WIDTH + blk_i * LANES
            blk = p[:, lo:lo + LANES]
            sq = blk * blk
            sq_hi = sq.astype(BF16)
            sq_lo = (sq - sq_hi.astype(F32)).astype(BF16)
            ssq = _dot(sq_hi, group_ones) + _dot(sq_lo, group_ones)
            rr = lax.rsqrt(ssq * (1.0 / DIFF_HALF) + EPS)
            y = _rope(blk * rr * gain, c3, s3, DIFF_HALF // 8, first3)
            out[:, blk_i * LANES:(blk_i + 1) * LANES] = y.astype(BF16)
    vc_o[...] = p[:, 2 * GROUP_WIDTH:].astype(BF16)

    p = proj(C_D, 3 * GROUP_WIDTH)
    qd_o[...] = (p[:, :GROUP_WIDTH] * HEAD_DIM ** -0.5).astype(BF16)
    kd_o[...] = p[:, GROUP_WIDTH:2 * GROUP_WIDTH].astype(BF16)
    vd_o[...] = p[:, 2 * GROUP_WIDTH:].astype(BF16)

    gate_o[...] = proj(C_GATE, 4 * GROUP_WIDTH).astype(BF16)


def _proj_call(xf, ng, w, wkv, gains, tables):
    rows, d_model = xf.shape
    tm = min(ROW_TILE, rows)
    row_spec = lambda width: pl.BlockSpec((tm, width), lambda i: (i, 0))
    full = lambda a: pl.BlockSpec(a.shape, lambda i: (0,) * a.ndim)
    bf = lambda width: jax.ShapeDtypeStruct((rows, width), BF16)
    out_shapes = (bf(QA_W), bf(LANES), bf(IDX_HEADS * IDX_DIM), bf(LANES),
                  jax.ShapeDtypeStruct((rows, LANES), F32),
                  bf(GROUP_WIDTH), bf(GROUP_WIDTH), bf(GROUP_WIDTH),
                  bf(GROUP_WIDTH), bf(GROUP_WIDTH), bf(GROUP_WIDTH),
                  bf(GROUP_WIDTH), bf(GROUP_WIDTH), bf(GROUP_WIDTH),
                  bf(4 * GROUP_WIDTH))
    return pl.pallas_call(
        _proj_kernel,
        out_shape=out_shapes,
        grid=(rows // tm,),
        in_specs=[row_spec(d_model), full(ng), full(w), full(wkv), full(gains)]
        + [row_spec(LANES)] * 6,
        out_specs=tuple(row_spec(s.shape[1]) for s in out_shapes),
        compiler_params=pltpu.CompilerParams(
            dimension_semantics=("parallel",), vmem_limit_bytes=VMEM_LIMIT),
        name="in_proj",
    )(xf, ng, w, wkv, gains, *tables)


def _dsa_kernel(qa_ref, qidx_ref, widx_ref, kidx_ref, kv_ref, o_ref,
                keys_ref, m_ref, l_ref, acc_ref, *, topk):
    t = qa_ref.shape[1]
    i = pl.program_id(1)
    nsub = t // LANES
    nblk = (i + 1) * nsub
    lane = _lane_iota((t, LANES))
    low_half = lane < IDX_DIM
    row = lax.broadcasted_iota(I32, (t, t), 0)
    col = lax.broadcasted_iota(I32, (t, t), 1)
    admissible = _group_of(col, CHUNK) <= _group_of(row, CHUNK)
    w = widx_ref[0]

    q_heads = []
    for hh in range(IDX_HEADS):
        blk = qidx_ref[0, :, (hh // 2) * LANES:(hh // 2 + 1) * LANES]
        keep = low_half if hh % 2 == 0 else jnp.logical_not(low_half)
        q_heads.append(jnp.where(keep, blk, jnp.zeros_like(blk)))

    def score_tile(j, diag):
        kt = kidx_ref[0, pl.ds(pl.multiple_of(j * t, t), t), :]
        sc = jnp.zeros((t, t), F32)
        for hh in range(IDX_HEADS):
            sc = sc + jnp.maximum(_dot_nt(q_heads[hh], kt), 0.0) * w[:, hh:hh + 1]
        sc = sc * (IDX_HEADS * IDX_DIM) ** -0.5
        bits = lax.bitcast_convert_type(sc, I32)
        key = bits ^ ((bits >> 31) & 0x7FFFFFFF)
        key = jnp.where(key == -1, 0, key)
        if diag:
            key = jnp.where(admissible, key, INT_MIN)
        for sub in range(nsub):
            keys_ref[j * nsub + sub] = key[:, sub * LANES:(sub + 1) * LANES]

    score_tile(i, True)

    def score_body(j, carry):
        score_tile(j, False)
        return carry

    lax.fori_loop(0, i, score_body, 0)

    def count(pred):
        def body(c, cnt):
            return cnt + jnp.where(pred(keys_ref[c], c), 1.0, 0.0)
        cnt = lax.fori_loop(0, nblk, body, jnp.zeros((t, LANES), F32))
        return jnp.sum(cnt, axis=1, keepdims=True)

    def bit_body(b, thr):
        cand = thr + lax.shift_left(jnp.int32(1), 31 - b)
        n_ge = count(lambda k, c: k >= cand)
        return jnp.where(n_ge >= topk, cand, thr)

    thr = lax.fori_loop(0, 32, bit_body, jnp.full((t, 1), INT_MIN, I32))
    thr = jnp.maximum(thr, INT_MIN + 1)

    n_ge = count(lambda k, c: k >= thr)
    n_gt = count(lambda k, c: k > thr)
    has_tie = n_ge > topk
    need = topk - n_gt

    @pl.when(jnp.max(jnp.where(has_tie, 1.0, 0.0)) > 0.0)
    def _():
        nbits = max(1, int(math.ceil(math.log2(keys_ref.shape[0] * LANES))) + 1)

        def idx_body(b, cut):
            cand = cut + lax.shift_left(jnp.int32(1), nbits - 1 - b)
            n_eq = count(lambda k, c: jnp.logical_and(k == thr, c * LANES + lane < cand))
            return jnp.where(n_eq <= need, cand, cut)

        cut = lax.fori_loop(0, nbits, idx_body, jnp.zeros((t, 1), I32))

        def fix_body(c, carry):
            k = keys_ref[c]
            drop = jnp.logical_and(jnp.logical_and(k == thr, c * LANES + lane >= cut), has_tie)
            keys_ref[c] = jnp.where(drop, INT_MIN, k)
            return carry

        lax.fori_loop(0, nblk, fix_body, 0)

    m_ref[...] = jnp.full(m_ref.shape, NEG_BIG, F32)
    l_ref[...] = jnp.zeros(l_ref.shape, F32)
    acc_ref[...] = jnp.zeros(acc_ref.shape, F32)

    def attn_body(j, carry):
        kvt = kv_ref[0, pl.ds(pl.multiple_of(j * t, t), t), :]
        sel = jnp.concatenate([keys_ref[j * nsub + sub] for sub in range(nsub)], axis=1) >= thr
        for hd in range(N_GROUP_HEADS):
            q = qa_ref[0, :, hd * LANES:(hd + 1) * LANES]
            s = _dot_nt(q, kvt)
            m_old = m_ref[hd]
            m_new = jnp.maximum(m_old, jnp.max(jnp.where(sel, s, NEG_BIG), axis=1, keepdims=True))
            p = jnp.where(sel, jnp.exp(s - m_new), 0.0)
            alpha = jnp.exp(m_old - m_new)
            l_ref[hd] = alpha * l_ref[hd] + jnp.sum(p, axis=1, keepdims=True)
            acc_ref[hd] = alpha * acc_ref[hd] + _dot(p.astype(BF16), kvt)
            m_ref[hd] = m_new
        return carry

    lax.fori_loop(0, i + 1, attn_body, 0)

    for pair in range(N_GROUP_HEADS // 2):
        o_even = acc_ref[2 * pair] / l_ref[2 * pair]
        o_odd = acc_ref[2 * pair + 1] / l_ref[2 * pair + 1]
        o_ref[0, :, pair * LANES:(pair + 1) * LANES] = jnp.where(
            low_half, pltpu.roll(o_even, HEAD_DIM, 1), o_odd).astype(o_ref.dtype)


def _dsa_call(qa, qidx, widx, kidx, kv, topk):
    b, s, _ = qa.shape
    t = min(SEQ_TILE, s)
    q_spec = lambda width: pl.BlockSpec((1, t, width), lambda bi, qi: (bi, qi, 0))
    kv_spec = pl.BlockSpec((1, s, LANES), lambda bi, qi: (bi, 0, 0))
    return pl.pallas_call(
        functools.partial(_dsa_kernel, topk=topk),
        out_shape=jax.ShapeDtypeStruct((b, s, GROUP_WIDTH), BF16),
        grid=(b, s // t),
        in_specs=[q_spec(QA_W), q_spec(IDX_HEADS * IDX_DIM), q_spec(LANES), kv_spec, kv_spec],
        out_specs=q_spec(GROUP_WIDTH),
        scratch_shapes=[pltpu.VMEM((s // LANES, t, LANES), I32),
                        pltpu.VMEM((N_GROUP_HEADS, t, 1), F32),
                        pltpu.VMEM((N_GROUP_HEADS, t, 1), F32),
                        pltpu.VMEM((N_GROUP_HEADS, t, LANES), F32)],
        compiler_params=pltpu.CompilerParams(
            dimension_semantics=("parallel", "arbitrary"), vmem_limit_bytes=VMEM_LIMIT),
        name="dsa",
    )(qa, qidx, widx, kidx, kv)


def _ret_kernel(g_ref, q_ref, k_ref, v_ref, o_ref, state_ref):
    t = q_ref.shape[1]
    ti = pl.program_id(1)

    @pl.when(ti == 0)
    def _():
        state_ref[...] = jnp.zeros(state_ref.shape, F32)

    lane = _lane_iota((t, LANES))
    low_half = lane < HEAD_DIM
    pos = lax.broadcasted_iota(I32, (t, LANES), 0).astype(F32)
    row = lax.broadcasted_iota(I32, (t, t), 0)
    col = lax.broadcasted_iota(I32, (t, t), 1)
    visible = _group_of(col, CHUNK) <= _group_of(row, CHUNK)
    dist = jnp.abs(row - col).astype(F32)
    srow = lax.broadcasted_iota(I32, (LANES, LANES), 0)
    scol = lax.broadcasted_iota(I32, (LANES, LANES), 1)
    same_head = _group_of(srow, HEAD_DIM) == _group_of(scol, HEAD_DIM)
    log_gamma = [math.log1p(-(2.0 ** (-5.0 - hd))) for hd in range(N_GROUP_HEADS)]

    for pair in range(N_GROUP_HEADS // 2):
        sl = slice(pair * LANES, (pair + 1) * LANES)
        qp, kp, vp = q_ref[0, :, sl], k_ref[0, :, sl], v_ref[0, :, sl]
        lg = jnp.where(low_half, log_gamma[2 * pair], log_gamma[2 * pair + 1])
        halves = []
        for hh in range(2):
            keep = low_half if hh == 0 else jnp.logical_not(low_half)
            qm = jnp.where(keep, qp, jnp.zeros_like(qp))
            decay = jnp.where(visible, jnp.exp(log_gamma[2 * pair + hh] * dist), 0.0)
            s = _dot_nt(qm, kp) * decay
            halves.append(_dot(s.astype(BF16), vp))
        y = jnp.where(low_half, halves[0], halves[1])
        state = state_ref[pair]
        q_dec = (qp.astype(F32) * jnp.exp(lg * (pos + 1.0))).astype(BF16)
        y = y + _dot(q_dec, state.astype(BF16))
        k_dec = (kp.astype(F32) * jnp.exp(lg * (t - 1.0 - pos))).astype(BF16)
        kv_new = jnp.where(same_head, _dot_tn(k_dec, vp), 0.0)
        state_ref[pair] = jnp.exp(lg[0:1, :] * float(t)) * state + kv_new

        inv = 1.0 / HEAD_DIM
        s_lo = jnp.sum(jnp.where(low_half, y, 0.0), axis=1, keepdims=True)
        s_all = jnp.sum(y, axis=1, keepdims=True)
        mu = jnp.where(low_half, s_lo, s_all - s_lo) * inv
        d = y - mu
        d2 = d * d
        v_lo = jnp.sum(jnp.where(low_half, d2, 0.0), axis=1, keepdims=True)
        v_all = jnp.sum(d2, axis=1, keepdims=True)
        var = jnp.where(low_half, v_lo, v_all - v_lo) * inv
        o_ref[0, :, sl] = (d * lax.rsqrt(var + EPS) * g_ref[:, sl]).astype(o_ref.dtype)


def _ret_call(g, q, k, v):
    b, s, _ = q.shape
    t = min(SEQ_TILE, s)
    spec = pl.BlockSpec((1, t, GROUP_WIDTH), lambda bi, ti: (bi, ti, 0))
    return pl.pallas_call(
        _ret_kernel,
        out_shape=jax.ShapeDtypeStruct((b, s, GROUP_WIDTH), BF16),
        grid=(b, s // t),
        in_specs=[pl.BlockSpec(g.shape, lambda bi, ti: (0, 0)), spec, spec, spec],
        out_specs=spec,
        scratch_shapes=[pltpu.VMEM((N_GROUP_HEADS // 2, LANES, LANES), F32)],
        compiler_params=pltpu.CompilerParams(
            dimension_semantics=("parallel", "arbitrary"), vmem_limit_bytes=VMEM_LIMIT),
        name="retention",
    )(g, q, k, v)


def _diff_kernel(lam_ref, g_ref, q_ref, k_ref, v_ref, o_ref, m_ref, l_ref, acc_ref, *, lam_init):
    t = q_ref.shape[1]
    i = pl.program_id(1)
    lane = _lane_iota((t, LANES))
    low_half = lane < HEAD_DIM
    row = lax.broadcasted_iota(I32, (t, t), 0)
    col = lax.broadcasted_iota(I32, (t, t), 1)
    visible = _group_of(col, CHUNK) <= _group_of(row, CHUNK)
    scale = DIFF_HALF ** -0.5

    lp = lam_ref[...]
    lam = (jnp.exp(jnp.sum(lp[0:1] * lp[1:2], axis=1, keepdims=True))
           - jnp.exp(jnp.sum(lp[2:3] * lp[3:4], axis=1, keepdims=True)) + lam_init)

    q_parts = []
    for hd in range(N_GROUP_HEADS):
        blk = q_ref[0, :, (hd // 2) * LANES:(hd // 2 + 1) * LANES]
        for which in range(2):
            lo = (hd % 2) * HEAD_DIM + which * DIFF_HALF
            keep = jnp.logical_and(lane >= lo, lane < lo + DIFF_HALF)
            q_parts.append(jnp.where(keep, blk, jnp.zeros_like(blk)))

    def tile(j, diag):
        start = pl.multiple_of(j * t, t)
        for hd in range(N_GROUP_HEADS):
            sl = slice((hd // 2) * LANES, (hd // 2 + 1) * LANES)
            kt = k_ref[0, pl.ds(start, t), sl]
            vt = v_ref[0, pl.ds(start, t), sl]
            for which in range(2):
                n = 2 * hd + which
                s = _dot_nt(q_parts[n], kt) * scale
                if diag:
                    s = jnp.where(visible, s, NEG_BIG)
                    m_new = jnp.max(s, axis=1, keepdims=True)
                    p = jnp.exp(s - m_new)
                    l_ref[n] = jnp.sum(p, axis=1, keepdims=True)
                    acc_ref[n] = _dot(p.astype(BF16), vt)
                else:
                    m_old = m_ref[n]
                    m_new = jnp.maximum(m_old, jnp.max(s, axis=1, keepdims=True))
                    p = jnp.exp(s - m_new)
                    alpha = jnp.exp(m_old - m_new)
                    l_ref[n] = alpha * l_ref[n] + jnp.sum(p, axis=1, keepdims=True)
                    acc_ref[n] = alpha * acc_ref[n] + _dot(p.astype(BF16), vt)
                m_ref[n] = m_new

    tile(i, True)

    def body(j, carry):
        tile(j, False)
        return carry

    lax.fori_loop(0, i, body, 0)

    inv = 1.0 / HEAD_DIM
    for pair in range(N_GROUP_HEADS // 2):
        heads = []
        for hh in range(2):
            n = 2 * (2 * pair + hh)
            heads.append(acc_ref[n] / l_ref[n] - lam * (acc_ref[n + 1] / l_ref[n + 1]))
        y = jnp.where(low_half, heads[0], heads[1])
        y2 = y * y
        s_lo = jnp.sum(jnp.where(low_half, y2, 0.0), axis=1, keepdims=True)
        s_all = jnp.sum(y2, axis=1, keepdims=True)
        ms = jnp.where(low_half, s_lo, s_all - s_lo) * inv
        out = y * lax.rsqrt(ms + EPS) * g_ref[...] * (1.0 - lam_init)
        o_ref[0, :, pair * LANES:(pair + 1) * LANES] = out.astype(o_ref.dtype)


def _diff_call(lam_params, g, q, k, v, lam_init):
    b, s, _ = q.shape
    t = min(SEQ_TILE, s)
    q_spec = pl.BlockSpec((1, t, GROUP_WIDTH), lambda bi, qi: (bi, qi, 0))
    kv_spec = pl.BlockSpec((1, s, GROUP_WIDTH), lambda bi, qi: (bi, 0, 0))
    small = lambda a: pl.BlockSpec(a.shape, lambda bi, qi: (0, 0))
    return pl.pallas_call(
        functools.partial(_diff_kernel, lam_init=lam_init),
        out_shape=jax.ShapeDtypeStruct((b, s, GROUP_WIDTH), BF16),
        grid=(b, s // t),
        in_specs=[small(lam_params), small(g), q_spec, kv_spec, kv_spec],
        out_specs=q_spec,
        scratch_shapes=[pltpu.VMEM((2 * N_GROUP_HEADS, t, 1), F32),
                        pltpu.VMEM((2 * N_GROUP_HEADS, t, 1), F32),
                        pltpu.VMEM((2 * N_GROUP_HEADS, t, LANES), F32)],
        compiler_params=pltpu.CompilerParams(
            dimension_semantics=("parallel", "arbitrary"), vmem_limit_bytes=VMEM_LIMIT),
        name="diff_attn",
    )(lam_params, g, q, k, v)


def _sb_kernel(q_ref, k_ref, v_ref, o_ref, rest_ref, acc_ref):
    t = q_ref.shape[1]
    i = pl.program_id(1)
    lane = _lane_iota((t, LANES))
    low_half = lane < HEAD_DIM
    row = lax.broadcasted_iota(I32, (t, t), 0)
    col = lax.broadcasted_iota(I32, (t, t), 1)
    strict = col < row
    suffix_ones = jnp.where(row >= col, 1.0, 0.0).astype(BF16)

    q_heads = []
    for hd in range(N_GROUP_HEADS):
        blk = q_ref[0, :, (hd // 2) * LANES:(hd // 2 + 1) * LANES]
        keep = low_half if hd % 2 == 0 else jnp.logical_not(low_half)
        q_heads.append(jnp.where(keep, blk, jnp.zeros_like(blk)))

    def tile(j, diag):
        start = pl.multiple_of(j * t, t)
        for hd in range(N_GROUP_HEADS):
            sl = slice((hd // 2) * LANES, (hd // 2 + 1) * LANES)
            kt = k_ref[0, pl.ds(start, t), sl]
            vt = v_ref[0, pl.ds(start, t), sl]
            z = _dot_nt(q_heads[hd], kt)
            log_1m = jnp.minimum(-z, 0.0) - jnp.log(1.0 + jnp.exp(-jnp.abs(z)))
            if diag:
                log_1m = jnp.where(strict, log_1m, 0.0)
                suffix = _dot(log_1m.astype(BF16), suffix_ones)
                a = jnp.where(strict, jnp.exp(z + suffix), 0.0)
                acc_ref[hd] = _dot(a.astype(BF16), vt)
                rest_ref[hd] = suffix[:, 0:1]
            else:
                suffix = _dot(log_1m.astype(BF16), suffix_ones)
                rest = rest_ref[hd]
                a = jnp.exp(z + suffix + rest)
                acc_ref[hd] = acc_ref[hd] + _dot(a.astype(BF16), vt)
                rest_ref[hd] = rest + suffix[:, 0:1]

    tile(i, True)

    def body(step, carry):
        tile(i - 1 - step, False)
        return carry

    lax.fori_loop(0, i, body, 0)

    for pair in range(N_GROUP_HEADS // 2):
        o_ref[0, :, pair * LANES:(pair + 1) * LANES] = jnp.where(
            low_half, acc_ref[2 * pair], acc_ref[2 * pair + 1]).astype(o_ref.dtype)


def _sb_call(q, k, v):
    b, s, _ = q.shape
    t = min(SEQ_TILE, s)
    q_spec = pl.BlockSpec((1, t, GROUP_WIDTH), lambda bi, qi: (bi, qi, 0))
    kv_spec = pl.BlockSpec((1, s, GROUP_WIDTH), lambda bi, qi: (bi, 0, 0))
    return pl.pallas_call(
        _sb_kernel,
        out_shape=jax.ShapeDtypeStruct((b, s, GROUP_WIDTH), BF16),
        grid=(b, s // t),
        in_specs=[q_spec, kv_spec, kv_spec],
        out_specs=q_spec,
        scratch_shapes=[pltpu.VMEM((N_GROUP_HEADS, t, 1), F32),
                        pltpu.VMEM((N_GROUP_HEADS, t, LANES), F32)],
        compiler_params=pltpu.CompilerParams(
            dimension_semantics=("parallel", "arbitrary"), vmem_limit_bytes=VMEM_LIMIT),
        name="stick_breaking",
    )(q, k, v)


def _out_kernel(x_ref, ya_ref, yb_ref, yc_ref, yd_ref, gate_ref, w_ref, o_ref, y_scr):
    for g, y_ref in enumerate((ya_ref, yb_ref, yc_ref, yd_ref)):
        sl = slice(g * GROUP_WIDTH, (g + 1) * GROUP_WIDTH)
        gate = gate_ref[:, sl].astype(F32)
        silu = gate / (1.0 + jnp.exp(-gate))
        y_scr[:, sl] = (y_ref[...].astype(F32) * silu).astype(BF16)
    o_ref[...] = x_ref[...] + _dot(y_scr[...], w_ref[...])


def _out_call(xf, ya, yb, yc, yd, gates, w):
    rows, d_model = xf.shape
    tm = min(ROW_TILE, rows)
    row_spec = lambda width: pl.BlockSpec((tm, width), lambda i: (i, 0))
    return pl.pallas_call(
        _out_kernel,
        out_shape=jax.ShapeDtypeStruct((rows, d_model), F32),
        grid=(rows // tm,),
        in_specs=[row_spec(d_model)] + [row_spec(GROUP_WIDTH)] * 4
        + [row_spec(4 * GROUP_WIDTH), pl.BlockSpec(w.shape, lambda i: (0, 0))],
        out_specs=row_spec(d_model),
        scratch_shapes=[pltpu.VMEM((tm, 4 * GROUP_WIDTH), BF16)],
        compiler_params=pltpu.CompilerParams(
            dimension_semantics=("parallel",), vmem_limit_bytes=VMEM_LIMIT),
        name="out_proj",
    )(xf, ya, yb, yc, yd, gates, w)


def _pack_w_in(w):
    d = w.shape[0]
    off = np.cumsum([0, GROUP_WIDTH, KV_RANK, IDX_HEADS * IDX_DIM, IDX_DIM, IDX_HEADS, GROUP_WIDTH]
                    + [GROUP_WIDTH] * 12)
    seg = lambda n: w[:, off[n]:off[n + 1]]
    zeros = lambda width: jnp.zeros((d, width), w.dtype)
    qa = seg(0)
    parts = []
    for hd in range(N_GROUP_HEADS):
        parts += [qa[:, hd * HEAD_DIM:(hd + 1) * HEAD_DIM], zeros(LANES - HEAD_DIM)]
    parts += [seg(1), seg(2), seg(3), seg(3), seg(4), zeros(LANES - IDX_HEADS)]
    parts += [seg(6), seg(7), seg(8)]
    parts += [seg(10), seg(11), seg(12)]
    parts += [seg(14), seg(15), seg(16)]
    parts += [seg(5), seg(9), seg(13), seg(17)]
    return jnp.concatenate(parts, axis=1).astype(BF16)


def _rope_tables(positions, theta, rot_dim, period):
    half = rot_dim // 2
    inv = theta ** (-jnp.arange(half, dtype=F32) * 2.0 / rot_dim)
    ang = positions.astype(F32)[..., None] * inv
    cos, sin = jnp.cos(ang), jnp.sin(ang)
    pad = period - rot_dim
    shape = ang.shape[:-1] + (pad,)
    cos_p = jnp.concatenate([cos, cos, jnp.ones(shape, F32)], axis=-1)
    sin_p = jnp.concatenate([-sin, sin, jnp.zeros(shape, F32)], axis=-1)
    reps = LANES // period
    cos_t = jnp.tile(cos_p, (1, 1, reps)).reshape(-1, LANES)
    sin_t = jnp.tile(sin_p, (1, 1, reps)).reshape(-1, LANES)
    return cos_t, sin_t


def _pad_lanes(v, fill):
    return jnp.concatenate([v.astype(F32), jnp.full((LANES - v.shape[0],), fill, F32)])


def kernel(x, positions, norm_g, w_in, kv_norm_g, w_kv_up, q_norm_a, k_norm_a, ret_norm_g,
           q_norm_c, k_norm_c, lam_q1, lam_k1, lam_q2, lam_k2, subln_g, w_out):
    b, s, d_model = x.shape
    depth = norm_g.shape[0]
    topk = min(TOPK_MAX, s // 4)
    assert s % SEQ_TILE == 0 or s < SEQ_TILE
    assert (b * s) % ROW_TILE == 0

    tables = (_rope_tables(positions, ROPE_THETA, HEAD_DIM // 4, HEAD_DIM)
              + _rope_tables(positions, RET_THETA, HEAD_DIM, HEAD_DIM)
              + _rope_tables(positions, ROPE_THETA, DIFF_HALF // 4, DIFF_HALF))

    xf = x.reshape(b * s, d_model)
    for l in range(depth):
        gains = jnp.stack([
            _pad_lanes(q_norm_a[l], 1.0),
            kv_norm_g[l].astype(F32),
            _pad_lanes(k_norm_a[l], 1.0),
            jnp.tile(q_norm_c[l].astype(F32), LANES // DIFF_HALF),
            jnp.tile(k_norm_c[l].astype(F32), LANES // DIFF_HALF),
            jnp.zeros((LANES,), F32), jnp.zeros((LANES,), F32), jnp.zeros((LANES,), F32)])
        (qa, kv, qidx, kidx, widx, qb, kb, vb, qc, kc, vc, qd, kd, vd, gates) = _proj_call(
            xf, norm_g[l][None, :].astype(F32), _pack_w_in(w_in[l]), w_kv_up[l].astype(BF16),
            gains, tables)

        seq = lambda a: a.reshape(b, s, a.shape[-1])
        y_a = _dsa_call(seq(qa), seq(qidx), seq(widx), seq(kidx), seq(kv), topk)
        y_b = _ret_call(ret_norm_g[l][None, :].astype(F32), seq(qb), seq(kb), seq(vb))
        lam_init = 0.8 - 0.6 * math.exp(-0.3 * l)
        lam_params = jnp.stack(
            [_pad_lanes(v[l], 0.0) for v in (lam_q1, lam_k1, lam_q2, lam_k2)]
            + [jnp.zeros((LANES,), F32)] * 4)
        subln = jnp.tile(subln_g[l].astype(F32), LANES // HEAD_DIM)[None, :]
        y_c = _diff_call(lam_params, subln, seq(qc), seq(kc), seq(vc), lam_init)
        y_d = _sb_call(seq(qd), seq(kd), seq(vd))

        flat = lambda a: a.reshape(b * s, a.shape[-1])
        xf = _out_call(xf, flat(y_a), flat(y_b), flat(y_c), flat(y_d), gates,
                       w_out[l].astype(BF16))
    return xf.reshape(b, s, d_model)
```

```python
import functools
import math

import numpy as np
import jax
import jax.numpy as jnp
from jax import lax
from jax.experimental import pallas as pl
from jax.experimental.pallas import tpu as pltpu

F32 = jnp.float32
BF16 = jnp.bfloat16
I32 = jnp.int32
I16 = jnp.int16

LANES = 128
CHUNK = 64
HEAD_DIM = 64
N_GROUP_HEADS = 4
GROUP_WIDTH = N_GROUP_HEADS * HEAD_DIM
KV_RANK = 128
IDX_HEADS = 8
IDX_DIM = 64
TOPK_MAX = 256
DIFF_HALF = HEAD_DIM // 2
ROPE_THETA = 500000.0
RET_THETA = 10000.0
EPS = 1e-6

ROW_TILE = 256
SEQ_TILE = ROW_TILE
COUNT_ROWS = 128
VMEM_LIMIT = 56 * 1024 * 1024
INT_MIN = -2 ** 31
I16_MIN = -2 ** 15
NEG_BIG = -1e30
LOG2E = math.log2(math.e)

QA_W = N_GROUP_HEADS * LANES
C_QA = 0
C_CKV = C_QA + QA_W
C_QIDX = C_CKV + KV_RANK
C_KIDX = C_QIDX + IDX_HEADS * IDX_DIM
C_B = C_KIDX + LANES
C_C = C_B + 3 * GROUP_WIDTH
C_D = C_C + 2 * GROUP_WIDTH
C_GATE = C_D + 2 * GROUP_WIDTH
PROJ_W = C_GATE + 4 * GROUP_WIDTH
T_WIDX = 0
T_VC = T_WIDX + LANES
T_VD = T_VC + GROUP_WIDTH
PROJ_T = T_VD + GROUP_WIDTH


def _dot(a, b):
    return jnp.dot(a, b, preferred_element_type=F32)


def _dot_nt(a, b):
    return lax.dot_general(a, b, (((1,), (1,)), ((), ())), preferred_element_type=F32)


def _dot_tn(a, b):
    return lax.dot_general(a, b, (((0,), (0,)), ((), ())), preferred_element_type=F32)


def _lane_iota(shape):
    return lax.broadcasted_iota(I32, shape, len(shape) - 1)


def _group_of(idx, size):
    return jnp.right_shift(idx, int(math.log2(size)))


def _rope(x, cos, sin, half, first_mask):
    partner = jnp.where(first_mask, pltpu.roll(x, LANES - half, 1), pltpu.roll(x, half, 1))
    return x * cos + partner * sin


def _skewed(n_chains, stages):
    for step in range(n_chains + max(lag for _, lag in stages)):
        for fn, lag in stages:
            if 0 <= step - lag < n_chains:
                fn(step - lag)


def _paired_tiles(n_tiles, tile_of, group):
    def body(it, carry):
        group([tile_of(2 * it), tile_of(2 * it + 1)])
        return carry

    lax.fori_loop(0, n_tiles // 2, body, 0)

    @pl.when(n_tiles % 2 == 1)
    def _():
        group([tile_of(n_tiles - 1)])


def _masked_lanes(blk, lo, width):
    lane = _lane_iota(blk.shape)
    keep = jnp.logical_and(lane >= lo, lane < lo + width)
    return jnp.where(keep, blk, jnp.zeros_like(blk))


def _proj_kernel(x_ref, ng_ref, w_ref, wt_ref, wkv_ref, gains_ref, c1_ref, s1_ref, c2_ref, s2_ref,
                 c3_ref, s3_ref,
                 qa_o, kv_o, kvt_o, qidx_o, kidx_o, widxt_o, qb_o, kb_o, vb_o, qc_o, kc_o, vct_o,
                 qd_o, kd_o, vdt_o, gate_o):
    tm = x_ref.shape[0]
    x = x_ref[...]
    r = lax.rsqrt(jnp.mean(x * x, axis=-1, keepdims=True) + EPS)
    h = (x * r * ng_ref[...]).astype(BF16)

    def proj(lo, width):
        return _dot(h, w_ref[:, lo:lo + width])

    def proj_t(lo, width):
        return _dot_nt(wt_ref[lo:lo + width, :], h)

    lane = _lane_iota((tm, LANES))
    low_half = lane < HEAD_DIM
    first1 = (lane & (HEAD_DIM - 1)) < HEAD_DIM // 8
    first2 = (lane & (HEAD_DIM - 1)) < HEAD_DIM // 2
    first3 = (lane & (DIFF_HALF - 1)) < DIFF_HALF // 8
    c1, s1 = c1_ref[...], s1_ref[...]
    c2, s2 = c2_ref[...], s2_ref[...]
    c3, s3 = c3_ref[...], s3_ref[...]
    g_qa = gains_ref[0:1, :]
    g_kv = gains_ref[1:2, :]
    g_ka = gains_ref[2:3, :]
    g_qc = gains_ref[3:4, :]
    g_kc = gains_ref[4:5, :]

    p = proj(C_QA, QA_W)
    for hd in range(N_GROUP_HEADS):
        blk = p[:, hd * LANES:(hd + 1) * LANES]
        rr = lax.rsqrt(jnp.sum(blk * blk, axis=-1, keepdims=True) * (1.0 / HEAD_DIM) + EPS)
        y = _rope(blk * rr * g_qa, c1, s1, HEAD_DIM // 8, first1)
        qa_o[:, hd * LANES:(hd + 1) * LANES] = (y * HEAD_DIM ** -0.5).astype(BF16)

    p = proj(C_CKV, KV_RANK)
    rr = lax.rsqrt(jnp.mean(p * p, axis=-1, keepdims=True) + EPS)
    ckv = (p * rr * g_kv).astype(BF16)
    kv = _dot(ckv, wkv_ref[...])
    ssq = jnp.sum(jnp.where(low_half, kv * kv, 0.0), axis=-1, keepdims=True)
    rk = lax.rsqrt(ssq * (1.0 / HEAD_DIM) + EPS)
    kr = _rope(kv * rk * g_ka, c1, s1, HEAD_DIM // 8, first1)
    kv = jnp.where(low_half, kr, kv)
    kv_o[...] = kv.astype(BF16)
    kvt_o[0] = kv.T.astype(BF16)

    p = proj(C_QIDX, IDX_HEADS * IDX_DIM)
    for blk_i in range(IDX_HEADS * IDX_DIM // LANES):
        blk = p[:, blk_i * LANES:(blk_i + 1) * LANES]
        qidx_o[:, blk_i * LANES:(blk_i + 1) * LANES] = _rope(
            blk, c1, s1, IDX_DIM // 8, first1).astype(BF16)
    p = proj(C_KIDX, LANES)
    rr = lax.rsqrt(jnp.sum(p * p, axis=-1, keepdims=True) * (0.5 / IDX_DIM) + EPS)
    kidx_o[...] = _rope(p * rr, c1, s1, IDX_DIM // 8, first1).astype(BF16)
    widxt_o[0] = proj_t(T_WIDX, LANES)[:IDX_HEADS, :]

    p = proj(C_B, 3 * GROUP_WIDTH)
    for blk_i in range(GROUP_WIDTH // LANES):
        sl = slice(blk_i * LANES, (blk_i + 1) * LANES)
        qb_o[:, sl] = _rope(p[:, sl], c2, s2, HEAD_DIM // 2, first2).astype(BF16)
        kblk = p[:, GROUP_WIDTH + blk_i * LANES:GROUP_WIDTH + (blk_i + 1) * LANES]
        kb_o[:, sl] = (_rope(kblk, c2, s2, HEAD_DIM // 2, first2) * HEAD_DIM ** -0.5).astype(BF16)
    vb_o[...] = p[:, 2 * GROUP_WIDTH:].astype(BF16)

    row = lax.broadcasted_iota(I32, (LANES, LANES), 0)
    col = lax.broadcasted_iota(I32, (LANES, LANES), 1)
    group_ones = jnp.where(_group_of(row, DIFF_HALF) == _group_of(col, DIFF_HALF),
                           1.0, 0.0).astype(BF16)
    p = proj(C_C, 2 * GROUP_WIDTH)
    for seg, (gain, out) in enumerate(((g_qc, qc_o), (g_kc, kc_o))):
        for blk_i in range(GROUP_WIDTH // LANES):
            lo = seg * GROUP_WIDTH + blk_i * LANES
            blk = p[:, lo:lo + LANES]
            sq = blk * blk
            sq_hi = sq.astype(BF16)
            sq_lo = (sq - sq_hi.astype(F32)).astype(BF16)
            ssq = _dot(sq_hi, group_ones) + _dot(sq_lo, group_ones)
            rr = lax.rsqrt(ssq * (1.0 / DIFF_HALF) + EPS)
            y = _rope(blk * rr * gain, c3, s3, DIFF_HALF // 8, first3)
            out[:, blk_i * LANES:(blk_i + 1) * LANES] = y.astype(BF16)
    vct_o[0] = proj_t(T_VC, GROUP_WIDTH).astype(BF16)

    p = proj(C_D, 2 * GROUP_WIDTH)
    qd_o[...] = (p[:, :GROUP_WIDTH] * HEAD_DIM ** -0.5).astype(BF16)
    kd_o[...] = p[:, GROUP_WIDTH:].astype(BF16)
    vdt_o[0] = proj_t(T_VD, GROUP_WIDTH).astype(BF16)

    gate_o[...] = proj(C_GATE, 4 * GROUP_WIDTH).astype(BF16)


def _proj_call(xf, ng, w, wt, wkv, gains, tables):
    rows, d_model = xf.shape
    tm = min(ROW_TILE, rows)
    row_spec = lambda width: pl.BlockSpec((tm, width), lambda i: (i, 0))
    full = lambda a: pl.BlockSpec(a.shape, lambda i: (0,) * a.ndim)
    bf = lambda width: jax.ShapeDtypeStruct((rows, width), BF16)
    tr = lambda feats, dt: jax.ShapeDtypeStruct((rows // tm, feats, tm), dt)
    out_shapes = (bf(QA_W), bf(LANES), tr(LANES, BF16), bf(IDX_HEADS * IDX_DIM), bf(LANES),
                  tr(IDX_HEADS, F32),
                  bf(GROUP_WIDTH), bf(GROUP_WIDTH), bf(GROUP_WIDTH),
                  bf(GROUP_WIDTH), bf(GROUP_WIDTH), tr(GROUP_WIDTH, BF16),
                  bf(GROUP_WIDTH), bf(GROUP_WIDTH), tr(GROUP_WIDTH, BF16),
                  bf(4 * GROUP_WIDTH))

    def out_spec(sds):
        if len(sds.shape) == 3:
            return pl.BlockSpec((1,) + sds.shape[1:], lambda i: (i, 0, 0))
        return row_spec(sds.shape[1])

    return pl.pallas_call(
        _proj_kernel,
        out_shape=out_shapes,
        grid=(rows // tm,),
        in_specs=[row_spec(d_model), full(ng), full(w), full(wt), full(wkv), full(gains)]
        + [row_spec(LANES)] * 6,
        out_specs=tuple(out_spec(s) for s in out_shapes),
        compiler_params=pltpu.CompilerParams(
            dimension_semantics=("parallel",), vmem_limit_bytes=VMEM_LIMIT),
        name="in_proj",
    )(xf, ng, w, wt, wkv, gains, *tables)


def _dsa_kernel(qa_ref, qidx_ref, widxt_ref, kidx_ref, kv_ref, kvt_ref, o_ref,
                keys_ref, half_ref, m_ref, l_ref, acc_ref, s_scr, p_scr, alpha_scr, *, topk):
    t = qa_ref.shape[1]
    i = pl.program_id(1)
    krow = lax.broadcasted_iota(I32, (t, t), 0)
    qcol = lax.broadcasted_iota(I32, (t, t), 1)
    admissible = _group_of(krow, CHUNK) <= _group_of(qcol, CHUNK)
    wt = widxt_ref[0, 0]

    q_heads = [_masked_lanes(qidx_ref[0, :, (hh // 2) * LANES:(hh // 2 + 1) * LANES],
                             (hh % 2) * IDX_DIM, IDX_DIM) for hh in range(IDX_HEADS)]

    def score_tile(j, diag):
        start = pl.multiple_of(j * t, t)
        kt = kidx_ref[0, pl.ds(start, t), :]
        sc = jnp.zeros((t, t), F32)
        for hh in range(IDX_HEADS):
            sc = sc + jnp.maximum(_dot_nt(kt, q_heads[hh]), 0.0) * wt[hh:hh + 1, :]
        sc = sc * (IDX_HEADS * IDX_DIM) ** -0.5
        bits = lax.bitcast_convert_type(sc, I32)
        key = bits ^ ((bits >> 31) & 0x7FFFFFFF)
        key = jnp.where(key == -1, 0, key)
        if diag:
            key = jnp.where(admissible, key, INT_MIN)
        keys_ref[pl.ds(start, t), :] = key
        half_ref[pl.ds(start, t), :] = (key >> 16).astype(I16)

    score_tile(i, True)

    def score_body(j, carry):
        score_tile(j, False)
        return carry

    lax.fori_loop(0, i, score_body, 0)

    nsteps = (i + 1) * (t // COUNT_ROWS)
    half = COUNT_ROWS // 2
    sub_row = lax.broadcasted_iota(I32, (half, t), 0)

    def count(pred):
        def body(c, cnt):
            base = pl.multiple_of(c * COUNT_ROWS, COUNT_ROWS)
            k_lo = keys_ref[pl.ds(base, half), :]
            k_hi = keys_ref[pl.ds(base + half, half), :]
            return (cnt + jnp.where(pred(k_lo, base + sub_row), 1.0, 0.0)
                    + jnp.where(pred(k_hi, base + half + sub_row), 1.0, 0.0))
        cnt = lax.fori_loop(0, nsteps, body, jnp.zeros((half, t), F32))
        return jnp.sum(cnt, axis=0, keepdims=True)

    def count_half(pred):
        one, zero = jnp.int16(1), jnp.int16(0)

        def body(c, cnt):
            base = pl.multiple_of(c * t, t)
            top = half_ref[pl.ds(base, t // 2), :]
            bot = half_ref[pl.ds(base + t // 2, t // 2), :]
            return cnt + jnp.where(pred(top), one, zero) + jnp.where(pred(bot), one, zero)

        cnt = lax.fori_loop(0, i + 1, body, jnp.zeros((t // 2, t), I16))
        return jnp.sum(cnt.astype(F32), axis=0, keepdims=True)

    def bisect_half(n_above):
        def bit_body(b, best):
            cand = best + lax.shift_left(jnp.int32(1), 15 - b)
            cand16 = cand.astype(I16)
            n_ge = n_above + count_half(lambda k: k >= cand16)
            return jnp.where(n_ge >= topk, cand, best)
        return lax.fori_loop(0, 16, bit_body, jnp.full((1, t), I16_MIN, I32))

    thr_hi = bisect_half(0.0)
    thr_hi16 = thr_hi.astype(I16)
    n_hi_gt = count_half(lambda k: k > thr_hi16)

    def low_half_body(c, carry):
        base = pl.multiple_of(c * t, t)
        k = keys_ref[pl.ds(base, t), :]
        low = (k & 0xFFFF) + I16_MIN
        half_ref[pl.ds(base, t), :] = jnp.where((k >> 16) == thr_hi, low, I16_MIN).astype(I16)
        return carry

    lax.fori_loop(0, i + 1, low_half_body, 0)
    thr_lo = bisect_half(n_hi_gt)
    thr = thr_hi * 65536 + (thr_lo - I16_MIN)
    thr = jnp.maximum(thr, INT_MIN + 1)

    n_ge = count(lambda k, idx: k >= thr)
    n_gt = count(lambda k, idx: k > thr)
    has_tie = n_ge > topk
    need = topk - n_gt

    @pl.when(jnp.max(jnp.where(has_tie, 1.0, 0.0)) > 0.0)
    def _():
        nbits = int(math.ceil(math.log2(keys_ref.shape[0]))) + 1

        def idx_body(b, cut):
            cand = cut + lax.shift_left(jnp.int32(1), nbits - 1 - b)
            n_eq = count(lambda k, idx: jnp.logical_and(k == thr, idx < cand))
            return jnp.where(n_eq <= need, cand, cut)

        cut = lax.fori_loop(0, nbits, idx_body, jnp.zeros((1, t), I32))
        full_row = lax.broadcasted_iota(I32, (COUNT_ROWS, t), 0)

        def fix_body(c, carry):
            base = pl.multiple_of(c * COUNT_ROWS, COUNT_ROWS)
            k = keys_ref[pl.ds(base, COUNT_ROWS), :]
            drop = jnp.logical_and(jnp.logical_and(k == thr, base + full_row >= cut), has_tie)
            keys_ref[pl.ds(base, COUNT_ROWS), :] = jnp.where(drop, INT_MIN, k)
            return carry

        lax.fori_loop(0, nsteps, fix_body, 0)

    m_ref[...] = jnp.full(m_ref.shape, NEG_BIG, F32)
    l_ref[...] = jnp.zeros(l_ref.shape, F32)
    acc_ref[...] = jnp.zeros(acc_ref.shape, F32)

    def attn_group(tiles):
        def split(c):
            return c, tiles[c // N_GROUP_HEADS], c % N_GROUP_HEADS

        def logits(c):
            c, j, hd = split(c)
            q = qa_ref[0, :, hd * LANES:(hd + 1) * LANES]
            s_scr[c] = _dot_nt(kv_ref[0, pl.ds(pl.multiple_of(j * t, t), t), :], q)

        def softmax(c):
            c, j, hd = split(c)
            sel = keys_ref[pl.ds(pl.multiple_of(j * t, t), t), :] >= thr
            s = jnp.where(sel, s_scr[c], NEG_BIG)
            m_old = m_ref[hd]
            m_new = jnp.maximum(m_old, jnp.max(s, axis=0, keepdims=True))
            p = jnp.exp2((s - m_new) * LOG2E)
            alpha = jnp.exp2((m_old - m_new) * LOG2E)
            l_ref[hd] = alpha * l_ref[hd] + jnp.sum(p, axis=0, keepdims=True)
            p_scr[c] = p.astype(BF16)
            alpha_scr[c] = alpha
            m_ref[hd] = m_new

        def weighted_values(c):
            c, j, hd = split(c)
            acc_ref[hd] = alpha_scr[c] * acc_ref[hd] + _dot(kvt_ref[0, j], p_scr[c])

        _skewed(len(tiles) * N_GROUP_HEADS, [(logits, 0), (softmax, 2), (weighted_values, 3)])

    _paired_tiles(i + 1, lambda n: n, attn_group)

    heads = [acc_ref[hd][HEAD_DIM:, :] / l_ref[hd] for hd in range(N_GROUP_HEADS)]
    o_ref[0] = jnp.concatenate(heads, axis=0).T.astype(o_ref.dtype)


def _dsa_call(qa, qidx, widxt, kidx, kv, kvt, topk):
    b, s, _ = qa.shape
    t = min(SEQ_TILE, s)
    q_spec = lambda width: pl.BlockSpec((1, t, width), lambda bi, qi: (bi, qi, 0))
    rows_spec = pl.BlockSpec((1, s, LANES), lambda bi, qi: (bi, 0, 0))
    return pl.pallas_call(
        functools.partial(_dsa_kernel, topk=topk),
        out_shape=jax.ShapeDtypeStruct((b, s, GROUP_WIDTH), BF16),
        grid=(b, s // t),
        in_specs=[q_spec(QA_W), q_spec(IDX_HEADS * IDX_DIM),
                  pl.BlockSpec((1, 1, IDX_HEADS, t), lambda bi, qi: (bi, qi, 0, 0)),
                  rows_spec, rows_spec,
                  pl.BlockSpec((1, s // t, LANES, t), lambda bi, qi: (bi, 0, 0, 0))],
        out_specs=q_spec(GROUP_WIDTH),
        scratch_shapes=[pltpu.VMEM((s, t), I32),
                        pltpu.VMEM((s, t), I16),
                        pltpu.VMEM((N_GROUP_HEADS, 1, t), F32),
                        pltpu.VMEM((N_GROUP_HEADS, 1, t), F32),
                        pltpu.VMEM((N_GROUP_HEADS, LANES, t), F32),
                        pltpu.VMEM((2 * N_GROUP_HEADS, t, t), F32),
                        pltpu.VMEM((2 * N_GROUP_HEADS, t, t), BF16),
                        pltpu.VMEM((2 * N_GROUP_HEADS, 1, t), F32)],
        compiler_params=pltpu.CompilerParams(
            dimension_semantics=("parallel", "arbitrary"), vmem_limit_bytes=VMEM_LIMIT),
        name="dsa",
    )(qa, qidx, widxt, kidx, kv, kvt)


def _ret_kernel(g_ref, q_ref, k_ref, v_ref, o_ref, state_ref):
    t = q_ref.shape[1]
    ti = pl.program_id(1)

    @pl.when(ti == 0)
    def _():
        state_ref[...] = jnp.zeros(state_ref.shape, F32)

    lane = _lane_iota((t, LANES))
    low_half = lane < HEAD_DIM
    pos = lax.broadcasted_iota(I32, (t, LANES), 0).astype(F32)
    row = lax.broadcasted_iota(I32, (t, t), 0)
    col = lax.broadcasted_iota(I32, (t, t), 1)
    visible = _group_of(col, CHUNK) <= _group_of(row, CHUNK)
    dist = jnp.abs(row - col).astype(F32)
    srow = lax.broadcasted_iota(I32, (LANES, LANES), 0)
    scol = lax.broadcasted_iota(I32, (LANES, LANES), 1)
    same_head = _group_of(srow, HEAD_DIM) == _group_of(scol, HEAD_DIM)
    log_gamma = [math.log1p(-(2.0 ** (-5.0 - hd))) for hd in range(N_GROUP_HEADS)]

    for pair in range(N_GROUP_HEADS // 2):
        sl = slice(pair * LANES, (pair + 1) * LANES)
        qp, kp, vp = q_ref[0, :, sl], k_ref[0, :, sl], v_ref[0, :, sl]
        lg = jnp.where(low_half, log_gamma[2 * pair], log_gamma[2 * pair + 1])
        halves = []
        for hh in range(2):
            keep = low_half if hh == 0 else jnp.logical_not(low_half)
            qm = jnp.where(keep, qp, jnp.zeros_like(qp))
            decay = jnp.where(visible, jnp.exp(log_gamma[2 * pair + hh] * dist), 0.0)
            s = _dot_nt(qm, kp) * decay
            halves.append(_dot(s.astype(BF16), vp))
        y = jnp.where(low_half, halves[0], halves[1])
        state = state_ref[pair]
        q_dec = (qp.astype(F32) * jnp.exp(lg * (pos + 1.0))).astype(BF16)
        y = y + _dot(q_dec, state.astype(BF16))
        k_dec = (kp.astype(F32) * jnp.exp(lg * (t - 1.0 - pos))).astype(BF16)
        kv_new = jnp.where(same_head, _dot_tn(k_dec, vp), 0.0)
        state_ref[pair] = jnp.exp(lg[0:1, :] * float(t)) * state + kv_new

        inv = 1.0 / HEAD_DIM
        s_lo = jnp.sum(jnp.where(low_half, y, 0.0), axis=1, keepdims=True)
        s_all = jnp.sum(y, axis=1, keepdims=True)
        mu = jnp.where(low_half, s_lo, s_all - s_lo) * inv
        d = y - mu
        d2 = d * d
        v_lo = jnp.sum(jnp.where(low_half, d2, 0.0), axis=1, keepdims=True)
        v_all = jnp.sum(d2, axis=1, keepdims=True)
        var = jnp.where(low_half, v_lo, v_all - v_lo) * inv
        o_ref[0, :, sl] = (d * lax.rsqrt(var + EPS) * g_ref[:, sl]).astype(o_ref.dtype)


def _ret_call(g, q, k, v):
    b, s, _ = q.shape
    t = min(SEQ_TILE, s)
    spec = pl.BlockSpec((1, t, GROUP_WIDTH), lambda bi, ti: (bi, ti, 0))
    return pl.pallas_call(
        _ret_kernel,
        out_shape=jax.ShapeDtypeStruct((b, s, GROUP_WIDTH), BF16),
        grid=(b, s // t),
        in_specs=[pl.BlockSpec(g.shape, lambda bi, ti: (0, 0)), spec, spec, spec],
        out_specs=spec,
        scratch_shapes=[pltpu.VMEM((N_GROUP_HEADS // 2, LANES, LANES), F32)],
        compiler_params=pltpu.CompilerParams(
            dimension_semantics=("parallel", "arbitrary"), vmem_limit_bytes=VMEM_LIMIT),
        name="retention",
    )(g, q, k, v)


def _diff_kernel(lam_ref, g_ref, q_ref, k_ref, vt_ref, o_ref, m_ref, l_ref, alpha_ref, acc_ref,
                 s_scr, p_scr, *, lam_init):
    t = q_ref.shape[1]
    i = pl.program_id(1)
    krow = lax.broadcasted_iota(I32, (t, t), 0)
    qcol = lax.broadcasted_iota(I32, (t, t), 1)
    visible = _group_of(krow, CHUNK) <= _group_of(qcol, CHUNK)
    exp_scale = DIFF_HALF ** -0.5 * LOG2E

    lp = lam_ref[...]
    lam = (jnp.exp(jnp.sum(lp[0:1] * lp[1:2], axis=1, keepdims=True))
           - jnp.exp(jnp.sum(lp[2:3] * lp[3:4], axis=1, keepdims=True)) + lam_init)

    q_parts = []
    for hd in range(N_GROUP_HEADS):
        blk = q_ref[0, :, (hd // 2) * LANES:(hd // 2 + 1) * LANES]
        for which in range(2):
            q_parts.append(_masked_lanes(blk, (hd % 2) * HEAD_DIM + which * DIFF_HALF, DIFF_HALF))

    m_ref[...] = jnp.full(m_ref.shape, NEG_BIG, F32)
    l_ref[...] = jnp.zeros(l_ref.shape, F32)
    acc_ref[...] = jnp.zeros(acc_ref.shape, F32)

    n_chains = 2 * N_GROUP_HEADS

    def tile(j, diag):
        start = pl.multiple_of(j * t, t)
        pair_sl = [slice(pr * LANES, (pr + 1) * LANES) for pr in range(N_GROUP_HEADS // 2)]

        def scores(n):
            s_scr[n] = _dot_nt(k_ref[0, pl.ds(start, t), pair_sl[n // 4]], q_parts[n])

        def softmax(n):
            s = s_scr[n]
            if diag:
                s = jnp.where(visible, s, NEG_BIG)
            m_old = m_ref[n]
            m_new = jnp.maximum(m_old, jnp.max(s, axis=0, keepdims=True))
            p = jnp.exp2((s - m_new) * exp_scale)
            alpha = jnp.exp2((m_old - m_new) * exp_scale)
            l_ref[n] = alpha * l_ref[n] + jnp.sum(p, axis=0, keepdims=True)
            p_scr[n] = p.astype(BF16)
            alpha_ref[n] = alpha
            m_ref[n] = m_new

        def weighted_values(n):
            acc_ref[n] = alpha_ref[n] * acc_ref[n] + _dot(vt_ref[0, j, pair_sl[n // 4], :], p_scr[n])

        for step in range(n_chains + 3):
            if step < n_chains:
                scores(step)
            if 0 <= step - 2 < n_chains:
                softmax(step - 2)
            if 0 <= step - 3 < n_chains:
                weighted_values(step - 3)

    tile(i, True)

    def body(j, carry):
        tile(j, False)
        return carry

    lax.fori_loop(0, i, body, 0)

    heads = []
    for hd in range(N_GROUP_HEADS):
        rows = slice((hd % 2) * HEAD_DIM, (hd % 2 + 1) * HEAD_DIM)
        y = (acc_ref[2 * hd][rows, :] / l_ref[2 * hd]
             - lam * (acc_ref[2 * hd + 1][rows, :] / l_ref[2 * hd + 1]))
        ms = jnp.mean(y * y, axis=0, keepdims=True)
        heads.append(y * lax.rsqrt(ms + EPS))
    out = jnp.concatenate(heads, axis=0).T * g_ref[...] * (1.0 - lam_init)
    o_ref[0] = out.astype(o_ref.dtype)


def _diff_call(lam_params, g, q, k, vt, lam_init):
    b, s, _ = q.shape
    t = min(SEQ_TILE, s)
    q_spec = pl.BlockSpec((1, t, GROUP_WIDTH), lambda bi, qi: (bi, qi, 0))
    k_spec = pl.BlockSpec((1, s, GROUP_WIDTH), lambda bi, qi: (bi, 0, 0))
    vt_spec = pl.BlockSpec((1, s // t, GROUP_WIDTH, t), lambda bi, qi: (bi, 0, 0, 0))
    small = lambda a: pl.BlockSpec(a.shape, lambda bi, qi: (0, 0))
    return pl.pallas_call(
        functools.partial(_diff_kernel, lam_init=lam_init),
        out_shape=jax.ShapeDtypeStruct((b, s, GROUP_WIDTH), BF16),
        grid=(b, s // t),
        in_specs=[small(lam_params), small(g), q_spec, k_spec, vt_spec],
        out_specs=q_spec,
        scratch_shapes=[pltpu.VMEM((2 * N_GROUP_HEADS, 1, t), F32),
                        pltpu.VMEM((2 * N_GROUP_HEADS, 1, t), F32),
                        pltpu.VMEM((2 * N_GROUP_HEADS, 1, t), F32),
                        pltpu.VMEM((2 * N_GROUP_HEADS, LANES, t), F32),
                        pltpu.VMEM((2 * N_GROUP_HEADS, t, t), F32),
                        pltpu.VMEM((2 * N_GROUP_HEADS, t, t), BF16)],
        compiler_params=pltpu.CompilerParams(
            dimension_semantics=("parallel", "arbitrary"), vmem_limit_bytes=VMEM_LIMIT),
        name="diff_attn",
    )(lam_params, g, q, k, vt)


def _sb_kernel(q_ref, k_ref, vt_ref, o_ref, rest_ref, acc_ref, z_scr, l_scr, suf_scr, a_scr):
    t = q_ref.shape[1]
    i = pl.program_id(1)
    krow = lax.broadcasted_iota(I32, (t, t), 0)
    qcol = lax.broadcasted_iota(I32, (t, t), 1)
    strict = krow < qcol
    suffix_ones = jnp.where(qcol >= krow, 1.0, 0.0).astype(BF16)

    q_heads = [_masked_lanes(q_ref[0, :, (hd // 2) * LANES:(hd // 2 + 1) * LANES],
                             (hd % 2) * HEAD_DIM, HEAD_DIM) for hd in range(N_GROUP_HEADS)]

    rest_ref[...] = jnp.zeros(rest_ref.shape, F32)
    acc_ref[...] = jnp.zeros(acc_ref.shape, F32)

    def group(tiles, diag=False):
        def split(c):
            j, hd = tiles[c // N_GROUP_HEADS], c % N_GROUP_HEADS
            return c, j, hd, slice((hd // 2) * LANES, (hd // 2 + 1) * LANES)

        def logits(c):
            c, j, hd, sl = split(c)
            z_scr[c] = _dot_nt(k_ref[0, pl.ds(pl.multiple_of(j * t, t), t), sl], q_heads[hd])

        def log_complement(c):
            z = z_scr[c]
            log_1m = jnp.minimum(-z, 0.0) - jnp.log(1.0 + jnp.exp(-jnp.abs(z)))
            if diag:
                log_1m = jnp.where(strict, log_1m, 0.0)
            l_scr[c] = log_1m.astype(BF16)

        def suffix_sums(c):
            suf_scr[c] = _dot(suffix_ones, l_scr[c])

        def weights(c):
            c, j, hd, sl = split(c)
            suffix = suf_scr[c]
            rest = rest_ref[hd]
            a = jnp.exp(z_scr[c] + suffix + rest)
            if diag:
                a = jnp.where(strict, a, 0.0)
            a_scr[c] = a.astype(BF16)
            rest_ref[hd] = rest + suffix[0:1, :]

        def weighted_values(c):
            c, j, hd, sl = split(c)
            acc_ref[hd] = acc_ref[hd] + _dot(vt_ref[0, j, sl, :], a_scr[c])

        _skewed(len(tiles) * N_GROUP_HEADS,
                [(logits, 0), (log_complement, 2), (suffix_sums, 3), (weights, 5),
                 (weighted_values, 6)])

    group([i], diag=True)
    _paired_tiles(i, lambda n: i - 1 - n, group)

    heads = [acc_ref[hd][(hd % 2) * HEAD_DIM:(hd % 2 + 1) * HEAD_DIM, :]
             for hd in range(N_GROUP_HEADS)]
    o_ref[0] = jnp.concatenate(heads, axis=0).T.astype(o_ref.dtype)


def _sb_call(q, k, vt):
    b, s, _ = q.shape
    t = min(SEQ_TILE, s)
    q_spec = pl.BlockSpec((1, t, GROUP_WIDTH), lambda bi, qi: (bi, qi, 0))
    k_spec = pl.BlockSpec((1, s, GROUP_WIDTH), lambda bi, qi: (bi, 0, 0))
    vt_spec = pl.BlockSpec((1, s // t, GROUP_WIDTH, t), lambda bi, qi: (bi, 0, 0, 0))
    return pl.pallas_call(
        _sb_kernel,
        out_shape=jax.ShapeDtypeStruct((b, s, GROUP_WIDTH), BF16),
        grid=(b, s // t),
        in_specs=[q_spec, k_spec, vt_spec],
        out_specs=q_spec,
        scratch_shapes=[pltpu.VMEM((N_GROUP_HEADS, 1, t), F32),
                        pltpu.VMEM((N_GROUP_HEADS, LANES, t), F32),
                        pltpu.VMEM((2 * N_GROUP_HEADS, t, t), F32),
                        pltpu.VMEM((2 * N_GROUP_HEADS, t, t), BF16),
                        pltpu.VMEM((2 * N_GROUP_HEADS, t, t), F32),
                        pltpu.VMEM((2 * N_GROUP_HEADS, t, t), BF16)],
        compiler_params=pltpu.CompilerParams(
            dimension_semantics=("parallel", "arbitrary"), vmem_limit_bytes=VMEM_LIMIT),
        name="stick_breaking",
    )(q, k, vt)


def _out_kernel(x_ref, ya_ref, yb_ref, yc_ref, yd_ref, gate_ref, w_ref, o_ref, y_scr):
    for g, y_ref in enumerate((ya_ref, yb_ref, yc_ref, yd_ref)):
        sl = slice(g * GROUP_WIDTH, (g + 1) * GROUP_WIDTH)
        gate = gate_ref[:, sl].astype(F32)
        silu = gate / (1.0 + jnp.exp(-gate))
        y_scr[:, sl] = (y_ref[...].astype(F32) * silu).astype(BF16)
    o_ref[...] = x_ref[...] + _dot(y_scr[...], w_ref[...])


def _out_call(xf, ya, yb, yc, yd, gates, w):
    rows, d_model = xf.shape
    tm = min(ROW_TILE, rows)
    row_spec = lambda width: pl.BlockSpec((tm, width), lambda i: (i, 0))
    return pl.pallas_call(
        _out_kernel,
        out_shape=jax.ShapeDtypeStruct((rows, d_model), F32),
        grid=(rows // tm,),
        in_specs=[row_spec(d_model)] + [row_spec(GROUP_WIDTH)] * 4
        + [row_spec(4 * GROUP_WIDTH), pl.BlockSpec(w.shape, lambda i: (0, 0))],
        out_specs=row_spec(d_model),
        scratch_shapes=[pltpu.VMEM((tm, 4 * GROUP_WIDTH), BF16)],
        compiler_params=pltpu.CompilerParams(
            dimension_semantics=("parallel",), vmem_limit_bytes=VMEM_LIMIT),
        name="out_proj",
    )(xf, ya, yb, yc, yd, gates, w)


def _pack_w_in(w):
    d = w.shape[0]
    off = np.cumsum([0, GROUP_WIDTH, KV_RANK, IDX_HEADS * IDX_DIM, IDX_DIM, IDX_HEADS, GROUP_WIDTH]
                    + [GROUP_WIDTH] * 12)
    seg = lambda n: w[:, off[n]:off[n + 1]]
    zeros = lambda width: jnp.zeros((d, width), w.dtype)
    qa = seg(0)
    parts = []
    for hd in range(N_GROUP_HEADS):
        parts += [qa[:, hd * HEAD_DIM:(hd + 1) * HEAD_DIM], zeros(LANES - HEAD_DIM)]
    parts += [seg(1), seg(2), seg(3), seg(3)]
    parts += [seg(6), seg(7), seg(8)]
    parts += [seg(10), seg(11)]
    parts += [seg(14), seg(15)]
    parts += [seg(5), seg(9), seg(13), seg(17)]
    packed = jnp.concatenate(parts, axis=1).astype(BF16)
    packed_t = jnp.concatenate([seg(4), zeros(LANES - IDX_HEADS), seg(12), seg(16)],
                               axis=1).T.astype(BF16)
    return packed, packed_t


def _rope_tables(positions, theta, rot_dim, period):
    half = rot_dim // 2
    inv = theta ** (-jnp.arange(half, dtype=F32) * 2.0 / rot_dim)
    ang = positions.astype(F32)[..., None] * inv
    cos, sin = jnp.cos(ang), jnp.sin(ang)
    pad = period - rot_dim
    shape = ang.shape[:-1] + (pad,)
    cos_p = jnp.concatenate([cos, cos, jnp.ones(shape, F32)], axis=-1)
    sin_p = jnp.concatenate([-sin, sin, jnp.zeros(shape, F32)], axis=-1)
    reps = LANES // period
    cos_t = jnp.tile(cos_p, (1, 1, reps)).reshape(-1, LANES)
    sin_t = jnp.tile(sin_p, (1, 1, reps)).reshape(-1, LANES)
    return cos_t, sin_t


def _pad_lanes(v, fill):
    return jnp.concatenate([v.astype(F32), jnp.full((LANES - v.shape[0],), fill, F32)])


def kernel(x, positions, norm_g, w_in, kv_norm_g, w_kv_up, q_norm_a, k_norm_a, ret_norm_g,
           q_norm_c, k_norm_c, lam_q1, lam_k1, lam_q2, lam_k2, subln_g, w_out):
    b, s, d_model = x.shape
    depth = norm_g.shape[0]
    topk = min(TOPK_MAX, s // 4)
    assert s % SEQ_TILE == 0 and (b * s) % ROW_TILE == 0
    nt = s // SEQ_TILE

    tables = (_rope_tables(positions, ROPE_THETA, HEAD_DIM // 4, HEAD_DIM)
              + _rope_tables(positions, RET_THETA, HEAD_DIM, HEAD_DIM)
              + _rope_tables(positions, ROPE_THETA, DIFF_HALF // 4, DIFF_HALF))

    xf = x.reshape(b * s, d_model)
    for l in range(depth):
        gains = jnp.stack([
            _pad_lanes(q_norm_a[l], 1.0),
            kv_norm_g[l].astype(F32),
            _pad_lanes(k_norm_a[l], 1.0),
            jnp.tile(q_norm_c[l].astype(F32), LANES // DIFF_HALF),
            jnp.tile(k_norm_c[l].astype(F32), LANES // DIFF_HALF),
            jnp.zeros((LANES,), F32), jnp.zeros((LANES,), F32), jnp.zeros((LANES,), F32)])
        w_packed, w_packed_t = _pack_w_in(w_in[l])
        (qa, kv, kvt, qidx, kidx, widxt, qb, kb, vb, qc, kc, vct, qd, kd, vdt, gates) = _proj_call(
            xf, norm_g[l][None, :].astype(F32), w_packed, w_packed_t, w_kv_up[l].astype(BF16),
            gains, tables)

        seq = lambda a: a.reshape(b, s, a.shape[-1])
        slabs = lambda a: a.reshape(b, nt, a.shape[-2], a.shape[-1])
        y_a = _dsa_call(seq(qa), seq(qidx), slabs(widxt), seq(kidx), seq(kv), slabs(kvt), topk)
        y_b = _ret_call(ret_norm_g[l][None, :].astype(F32), seq(qb), seq(kb), seq(vb))
        lam_init = 0.8 - 0.6 * math.exp(-0.3 * l)
        lam_params = jnp.stack(
            [_pad_lanes(v[l], 0.0) for v in (lam_q1, lam_k1, lam_q2, lam_k2)]
            + [jnp.zeros((LANES,), F32)] * 4)
        subln = jnp.tile(subln_g[l].astype(F32), N_GROUP_HEADS)[None, :]
        y_c = _diff_call(lam_params, subln, seq(qc), seq(kc), slabs(vct), lam_init)
        y_d = _sb_call(seq(qd), seq(kd), slabs(vdt))

        flat = lambda a: a.reshape(b * s, a.shape[-1])
        xf = _out_call(xf, flat(y_a), flat(y_b), flat(y_c), flat(y_d), gates,
                       w_out[l].astype(BF16))
    return xf.reshape(b, s, d_model)
```

```python
import functools
import math

import numpy as np
import jax
import jax.numpy as jnp
from jax import lax
from jax.experimental import pallas as pl
from jax.experimental.pallas import tpu as pltpu

F32 = jnp.float32
BF16 = jnp.bfloat16
I32 = jnp.int32
I16 = jnp.int16

LANES = 128
CHUNK = 64
HEAD_DIM = 64
N_GROUP_HEADS = 4
GROUP_WIDTH = N_GROUP_HEADS * HEAD_DIM
KV_RANK = 128
IDX_HEADS = 8
IDX_DIM = 64
TOPK_MAX = 256
DIFF_HALF = HEAD_DIM // 2
ROPE_THETA = 500000.0
RET_THETA = 10000.0
EPS = 1e-6

ROW_TILE = 256
SEQ_TILE = ROW_TILE
COUNT_ROWS = 128
VMEM_LIMIT = 56 * 1024 * 1024
INT_MIN = -2 ** 31
I16_MIN = -2 ** 15
NEG_BIG = -1e30
ONES_ROWS = 16
SOFTPLUS_LINEAR = 30.0
SB_DEAD_LOG = -104.0
LOG2E = math.log2(math.e)

QA_W = N_GROUP_HEADS * LANES
C_QA = 0
C_CKV = C_QA + QA_W
C_QIDX = C_CKV + KV_RANK
C_KIDX = C_QIDX + IDX_HEADS * IDX_DIM
C_B = C_KIDX + LANES
C_C = C_B + 3 * GROUP_WIDTH
C_D = C_C + 2 * GROUP_WIDTH
C_GATE = C_D + 2 * GROUP_WIDTH
PROJ_W = C_GATE + 4 * GROUP_WIDTH
T_WIDX = 0
T_VC = T_WIDX + LANES
T_VD = T_VC + GROUP_WIDTH
PROJ_T = T_VD + GROUP_WIDTH


def _dot(a, b):
    return jnp.dot(a, b, preferred_element_type=F32)


def _dot_nt(a, b):
    return lax.dot_general(a, b, (((1,), (1,)), ((), ())), preferred_element_type=F32)


def _dot_tn(a, b):
    return lax.dot_general(a, b, (((0,), (0,)), ((), ())), preferred_element_type=F32)


def _lane_iota(shape):
    return lax.broadcasted_iota(I32, shape, len(shape) - 1)


def _group_of(idx, size):
    return jnp.right_shift(idx, int(math.log2(size)))


def _rope(x, cos, sin, half, first_mask):
    partner = jnp.where(first_mask, pltpu.roll(x, LANES - half, 1), pltpu.roll(x, half, 1))
    return x * cos + partner * sin


def _skewed(n_chains, stages):
    for step in range(n_chains + max(lag for _, lag in stages)):
        for fn, lag in stages:
            if 0 <= step - lag < n_chains:
                fn(step - lag)


def _tile_spans(n_tiles, span):
    def body(it, carry):
        span(2 * it, 2)
        return carry

    lax.fori_loop(0, n_tiles // 2, body, 0)

    @pl.when(n_tiles % 2 == 1)
    def _():
        span(n_tiles - 1, 1)


def _masked_lanes(blk, lo, width):
    lane = _lane_iota(blk.shape)
    keep = jnp.logical_and(lane >= lo, lane < lo + width)
    return jnp.where(keep, blk, jnp.zeros_like(blk))


def _proj_kernel(x_ref, ng_ref, w_ref, wt_ref, wkv_ref, gains_ref, c1_ref, s1_ref, c2_ref, s2_ref,
                 c3_ref, s3_ref,
                 qa_o, kv_o, kvt_o, qidx_o, kidx_o, widxt_o, qb_o, kb_o, vb_o, qc_o, kc_o, vct_o,
                 qd_o, kd_o, vdt_o, gate_o):
    tm = x_ref.shape[0]
    x = x_ref[...]
    r = lax.rsqrt(jnp.mean(x * x, axis=-1, keepdims=True) + EPS)
    h = (x * r * ng_ref[...]).astype(BF16)

    def proj(lo, width):
        return _dot(h, w_ref[:, lo:lo + width])

    def proj_t(lo, width):
        return _dot_nt(wt_ref[lo:lo + width, :], h)

    lane = _lane_iota((tm, LANES))
    low_half = lane < HEAD_DIM
    first1 = (lane & (HEAD_DIM - 1)) < HEAD_DIM // 8
    first2 = (lane & (HEAD_DIM - 1)) < HEAD_DIM // 2
    first3 = (lane & (DIFF_HALF - 1)) < DIFF_HALF // 8
    c1, s1 = c1_ref[...], s1_ref[...]
    c2, s2 = c2_ref[...], s2_ref[...]
    c3, s3 = c3_ref[...], s3_ref[...]
    g_qa = gains_ref[0:1, :]
    g_kv = gains_ref[1:2, :]
    g_ka = gains_ref[2:3, :]
    g_qc = gains_ref[3:4, :]
    g_kc = gains_ref[4:5, :]

    p = proj(C_QA, QA_W)
    for hd in range(N_GROUP_HEADS):
        blk = p[:, hd * LANES:(hd + 1) * LANES]
        rr = lax.rsqrt(jnp.sum(blk * blk, axis=-1, keepdims=True) * (1.0 / HEAD_DIM) + EPS)
        y = _rope(blk * rr * g_qa, c1, s1, HEAD_DIM // 8, first1)
        qa_o[:, hd * LANES:(hd + 1) * LANES] = (y * HEAD_DIM ** -0.5).astype(BF16)

    p = proj(C_CKV, KV_RANK)
    rr = lax.rsqrt(jnp.mean(p * p, axis=-1, keepdims=True) + EPS)
    ckv = (p * rr * g_kv).astype(BF16)
    kv = _dot(ckv, wkv_ref[...])
    ssq = jnp.sum(jnp.where(low_half, kv * kv, 0.0), axis=-1, keepdims=True)
    rk = lax.rsqrt(ssq * (1.0 / HEAD_DIM) + EPS)
    kr = _rope(kv * rk * g_ka, c1, s1, HEAD_DIM // 8, first1)
    kv = jnp.where(low_half, kr, kv)
    kv_o[...] = kv.astype(BF16)
    kvt_o[0, :ONES_ROWS, :] = jnp.ones((ONES_ROWS, tm), BF16)
    kvt_o[0, ONES_ROWS:, :] = kv.T[HEAD_DIM:, :].astype(BF16)

    p = proj(C_QIDX, IDX_HEADS * IDX_DIM)
    for blk_i in range(IDX_HEADS * IDX_DIM // LANES):
        blk = p[:, blk_i * LANES:(blk_i + 1) * LANES]
        qidx_o[:, blk_i * LANES:(blk_i + 1) * LANES] = _rope(
            blk, c1, s1, IDX_DIM // 8, first1).astype(BF16)
    p = proj(C_KIDX, LANES)
    rr = lax.rsqrt(jnp.sum(p * p, axis=-1, keepdims=True) * (0.5 / IDX_DIM) + EPS)
    kidx_o[...] = _rope(p * rr, c1, s1, IDX_DIM // 8, first1).astype(BF16)
    widxt_o[0] = proj_t(T_WIDX, LANES)[:IDX_HEADS, :]

    p = proj(C_B, 3 * GROUP_WIDTH)
    for blk_i in range(GROUP_WIDTH // LANES):
        sl = slice(blk_i * LANES, (blk_i + 1) * LANES)
        qb_o[:, sl] = _rope(p[:, sl], c2, s2, HEAD_DIM // 2, first2).astype(BF16)
        kblk = p[:, GROUP_WIDTH + blk_i * LANES:GROUP_WIDTH + (blk_i + 1) * LANES]
        kb_o[:, sl] = (_rope(kblk, c2, s2, HEAD_DIM // 2, first2) * HEAD_DIM ** -0.5).astype(BF16)
    vb_o[...] = p[:, 2 * GROUP_WIDTH:].astype(BF16)

    row = lax.broadcasted_iota(I32, (LANES, LANES), 0)
    col = lax.broadcasted_iota(I32, (LANES, LANES), 1)
    group_ones = jnp.where(_group_of(row, DIFF_HALF) == _group_of(col, DIFF_HALF),
                           1.0, 0.0).astype(BF16)
    p = proj(C_C, 2 * GROUP_WIDTH)
    for seg, (gain, out) in enumerate(((g_qc, qc_o), (g_kc, kc_o))):
        for blk_i in range(GROUP_WIDTH // LANES):
            lo = seg * GROUP_WIDTH + blk_i * LANES
            blk = p[:, lo:lo + LANES]
            sq = blk * blk
            sq_hi = sq.astype(BF16)
            sq_lo = (sq - sq_hi.astype(F32)).astype(BF16)
            ssq = _dot(sq_hi, group_ones) + _dot(sq_lo, group_ones)
            rr = lax.rsqrt(ssq * (1.0 / DIFF_HALF) + EPS)
            y = _rope(blk * rr * gain, c3, s3, DIFF_HALF // 8, first3)
            out[:, blk_i * LANES:(blk_i + 1) * LANES] = y.astype(BF16)
    vct_o[0] = proj_t(T_VC, GROUP_WIDTH).astype(BF16)

    p = proj(C_D, 2 * GROUP_WIDTH)
    qd_o[...] = (p[:, :GROUP_WIDTH] * HEAD_DIM ** -0.5).astype(BF16)
    kd_o[...] = p[:, GROUP_WIDTH:].astype(BF16)
    vdt_o[0] = proj_t(T_VD, GROUP_WIDTH).astype(BF16)

    gate_o[...] = proj(C_GATE, 4 * GROUP_WIDTH).astype(BF16)


def _proj_call(xf, ng, w, wt, wkv, gains, tables):
    rows, d_model = xf.shape
    tm = min(ROW_TILE, rows)
    row_spec = lambda width: pl.BlockSpec((tm, width), lambda i: (i, 0))
    full = lambda a: pl.BlockSpec(a.shape, lambda i: (0,) * a.ndim)
    bf = lambda width: jax.ShapeDtypeStruct((rows, width), BF16)
    tr = lambda feats, dt: jax.ShapeDtypeStruct((rows // tm, feats, tm), dt)
    out_shapes = (bf(QA_W), bf(LANES), tr(ONES_ROWS + HEAD_DIM, BF16), bf(IDX_HEADS * IDX_DIM),
                  bf(LANES),
                  tr(IDX_HEADS, F32),
                  bf(GROUP_WIDTH), bf(GROUP_WIDTH), bf(GROUP_WIDTH),
                  bf(GROUP_WIDTH), bf(GROUP_WIDTH), tr(GROUP_WIDTH, BF16),
                  bf(GROUP_WIDTH), bf(GROUP_WIDTH), tr(GROUP_WIDTH, BF16),
                  bf(4 * GROUP_WIDTH))

    def out_spec(sds):
        if len(sds.shape) == 3:
            return pl.BlockSpec((1,) + sds.shape[1:], lambda i: (i, 0, 0))
        return row_spec(sds.shape[1])

    return pl.pallas_call(
        _proj_kernel,
        out_shape=out_shapes,
        grid=(rows // tm,),
        in_specs=[row_spec(d_model), full(ng), full(w), full(wt), full(wkv), full(gains)]
        + [row_spec(LANES)] * 6,
        out_specs=tuple(out_spec(s) for s in out_shapes),
        compiler_params=pltpu.CompilerParams(
            dimension_semantics=("parallel",), vmem_limit_bytes=VMEM_LIMIT),
        name="in_proj",
    )(xf, ng, w, wt, wkv, gains, *tables)


def _dsa_kernel(qa_ref, qidx_ref, widxt_ref, kidx_ref, kv_ref, kvt_ref, o_ref,
                keys_ref, half_ref, m_ref, acc_ref, s_scr, p_scr, alpha_scr, smax_scr, *, topk):
    t = qa_ref.shape[1]
    i = pl.program_id(1)
    krow = lax.broadcasted_iota(I32, (t, t), 0)
    qcol = lax.broadcasted_iota(I32, (t, t), 1)
    admissible = _group_of(krow, CHUNK) <= _group_of(qcol, CHUNK)
    wt = widxt_ref[0, 0]

    q_heads = [_masked_lanes(qidx_ref[0, :, (hh // 2) * LANES:(hh // 2 + 1) * LANES],
                             (hh % 2) * IDX_DIM, IDX_DIM) for hh in range(IDX_HEADS)]

    def score_span(j0, ntile, diag=False):
        rows = ntile * t
        start = pl.multiple_of(j0 * t, t)
        kt = kidx_ref[0, pl.ds(start, rows), :]
        sc = jnp.zeros((rows, t), F32)
        for hh in range(IDX_HEADS):
            sc = sc + jnp.maximum(_dot_nt(kt, q_heads[hh]), 0.0) * wt[hh:hh + 1, :]
        sc = sc * (IDX_HEADS * IDX_DIM) ** -0.5
        bits = lax.bitcast_convert_type(sc, I32)
        key = bits ^ ((bits >> 31) & 0x7FFFFFFF)
        key = jnp.where(key == -1, 0, key)
        if diag:
            key = jnp.where(admissible, key, INT_MIN)
        keys_ref[pl.ds(start, rows), :] = key
        half_ref[pl.ds(start, rows), :] = (key >> 16).astype(I16)

    score_span(i, 1, diag=True)
    _tile_spans(i, score_span)

    nsteps = (i + 1) * (t // COUNT_ROWS)
    half = COUNT_ROWS // 2
    sub_row = lax.broadcasted_iota(I32, (half, t), 0)

    def count(pred):
        def body(c, cnt):
            base = pl.multiple_of(c * COUNT_ROWS, COUNT_ROWS)
            k_lo = keys_ref[pl.ds(base, half), :]
            k_hi = keys_ref[pl.ds(base + half, half), :]
            return (cnt + jnp.where(pred(k_lo, base + sub_row), 1.0, 0.0)
                    + jnp.where(pred(k_hi, base + half + sub_row), 1.0, 0.0))
        cnt = lax.fori_loop(0, nsteps, body, jnp.zeros((half, t), F32))
        return jnp.sum(cnt, axis=0, keepdims=True)

    def count_half(pred):
        one, zero = jnp.int16(1), jnp.int16(0)

        def body(c, cnt):
            base = pl.multiple_of(c * t, t)
            top = half_ref[pl.ds(base, t // 2), :]
            bot = half_ref[pl.ds(base + t // 2, t // 2), :]
            return cnt + jnp.where(pred(top), one, zero) + jnp.where(pred(bot), one, zero)

        cnt = lax.fori_loop(0, i + 1, body, jnp.zeros((t // 2, t), I16))
        return jnp.sum(cnt.astype(F32), axis=0, keepdims=True)

    def bisect_half(n_above, n_all):
        def bit_body(b, carry):
            best, n_best = carry
            cand = best + lax.shift_left(jnp.int32(1), 15 - b)
            cand16 = cand.astype(I16)
            n_ge = n_above + count_half(lambda k: k >= cand16)
            ok = n_ge >= topk
            return jnp.where(ok, cand, best), jnp.where(ok, n_ge, n_best)
        return lax.fori_loop(0, 16, bit_body, (jnp.full((1, t), I16_MIN, I32), n_all))

    n_rows = jnp.full((1, t), 1.0, F32) * ((i + 1) * t).astype(F32)
    thr_hi, n_hi_ge = bisect_half(0.0, n_rows)
    thr_hi16 = thr_hi.astype(I16)
    n_hi_gt = count_half(lambda k: k > thr_hi16)

    def low_half_body(c, carry):
        base = pl.multiple_of(c * t, t)
        k = keys_ref[pl.ds(base, t), :]
        low = (k & 0xFFFF) + I16_MIN
        half_ref[pl.ds(base, t), :] = jnp.where((k >> 16) == thr_hi, low, I16_MIN).astype(I16)
        return carry

    lax.fori_loop(0, i + 1, low_half_body, 0)
    thr_lo, n_ge = bisect_half(n_hi_gt, n_hi_ge)
    thr = thr_hi * 65536 + (thr_lo - I16_MIN)
    has_tie = jnp.logical_and(n_ge > topk, thr > INT_MIN)
    thr = jnp.maximum(thr, INT_MIN + 1)

    @pl.when(jnp.max(jnp.where(has_tie, 1.0, 0.0)) > 0.0)
    def _():
        nbits = int(math.ceil(math.log2(keys_ref.shape[0]))) + 1
        need = topk - count(lambda k, idx: k > thr)

        def idx_body(b, cut):
            cand = cut + lax.shift_left(jnp.int32(1), nbits - 1 - b)
            n_eq = count(lambda k, idx: jnp.logical_and(k == thr, idx < cand))
            return jnp.where(n_eq <= need, cand, cut)

        cut = lax.fori_loop(0, nbits, idx_body, jnp.zeros((1, t), I32))
        full_row = lax.broadcasted_iota(I32, (COUNT_ROWS, t), 0)

        def fix_body(c, carry):
            base = pl.multiple_of(c * COUNT_ROWS, COUNT_ROWS)
            k = keys_ref[pl.ds(base, COUNT_ROWS), :]
            drop = jnp.logical_and(jnp.logical_and(k == thr, base + full_row >= cut), has_tie)
            keys_ref[pl.ds(base, COUNT_ROWS), :] = jnp.where(drop, INT_MIN, k)
            return carry

        lax.fori_loop(0, nsteps, fix_body, 0)

    m_ref[...] = jnp.full(m_ref.shape, NEG_BIG, F32)
    acc_ref[...] = jnp.zeros(acc_ref.shape, F32)

    def attn_span(j0, ntile):
        rows = ntile * t
        start = pl.multiple_of(j0 * t, t)

        def logits(hd):
            q = qa_ref[0, :, hd * LANES:(hd + 1) * LANES]
            sel = keys_ref[pl.ds(start, rows), :] >= thr
            s = jnp.where(sel, _dot_nt(kv_ref[0, pl.ds(start, rows), :], q), NEG_BIG)
            s_scr[hd, :rows] = s
            smax_scr[hd] = jnp.max(s, axis=0, keepdims=True)

        def softmax(hd):
            m_old = m_ref[hd]
            m_new = jnp.maximum(m_old, smax_scr[hd])
            p_scr[hd, :rows] = jnp.exp2((s_scr[hd, :rows] - m_new) * LOG2E).astype(BF16)
            alpha_scr[hd] = jnp.exp2((m_old - m_new) * LOG2E)
            m_ref[hd] = m_new

        def weighted_values(hd):
            pv = _dot(kvt_ref[0, j0], p_scr[hd, :t])
            if ntile == 2:
                pv = pv + _dot(kvt_ref[0, j0 + 1], p_scr[hd, t:])
            acc_ref[hd] = alpha_scr[hd] * acc_ref[hd] + pv

        _skewed(N_GROUP_HEADS, [(logits, 0), (softmax, 2), (weighted_values, 3)])

    _tile_spans(i + 1, attn_span)

    heads = [acc_ref[hd][ONES_ROWS:, :] / acc_ref[hd][0:1, :] for hd in range(N_GROUP_HEADS)]
    o_ref[0] = jnp.concatenate(heads, axis=0).T.astype(o_ref.dtype)


def _dsa_call(qa, qidx, widxt, kidx, kv, kvt, topk):
    b, s, _ = qa.shape
    t = min(SEQ_TILE, s)
    q_spec = lambda width: pl.BlockSpec((1, t, width), lambda bi, qi: (bi, qi, 0))
    rows_spec = pl.BlockSpec((1, s, LANES), lambda bi, qi: (bi, 0, 0))
    return pl.pallas_call(
        functools.partial(_dsa_kernel, topk=topk),
        out_shape=jax.ShapeDtypeStruct((b, s, GROUP_WIDTH), BF16),
        grid=(b, s // t),
        in_specs=[q_spec(QA_W), q_spec(IDX_HEADS * IDX_DIM),
                  pl.BlockSpec((1, 1, IDX_HEADS, t), lambda bi, qi: (bi, qi, 0, 0)),
                  rows_spec, rows_spec,
                  pl.BlockSpec((1, s // t, ONES_ROWS + HEAD_DIM, t), lambda bi, qi: (bi, 0, 0, 0))],
        out_specs=q_spec(GROUP_WIDTH),
        scratch_shapes=[pltpu.VMEM((s, t), I32),
                        pltpu.VMEM((s, t), I16),
                        pltpu.VMEM((N_GROUP_HEADS, 1, t), F32),
                        pltpu.VMEM((N_GROUP_HEADS, ONES_ROWS + HEAD_DIM, t), F32),
                        pltpu.VMEM((N_GROUP_HEADS, 2 * t, t), F32),
                        pltpu.VMEM((N_GROUP_HEADS, 2 * t, t), BF16),
                        pltpu.VMEM((N_GROUP_HEADS, 1, t), F32),
                        pltpu.VMEM((N_GROUP_HEADS, 1, t), F32)],
        compiler_params=pltpu.CompilerParams(
            dimension_semantics=("parallel", "arbitrary"), vmem_limit_bytes=VMEM_LIMIT),
        name="dsa",
    )(qa, qidx, widxt, kidx, kv, kvt)


def _ret_kernel(g_ref, q_ref, k_ref, v_ref, o_ref, state_ref):
    t = q_ref.shape[1]
    ti = pl.program_id(1)

    @pl.when(ti == 0)
    def _():
        state_ref[...] = jnp.zeros(state_ref.shape, F32)

    lane = _lane_iota((t, LANES))
    low_half = lane < HEAD_DIM
    pos = lax.broadcasted_iota(I32, (t, LANES), 0).astype(F32)
    row = lax.broadcasted_iota(I32, (t, t), 0)
    col = lax.broadcasted_iota(I32, (t, t), 1)
    visible = _group_of(col, CHUNK) <= _group_of(row, CHUNK)
    dist = jnp.abs(row - col).astype(F32)
    srow = lax.broadcasted_iota(I32, (LANES, LANES), 0)
    scol = lax.broadcasted_iota(I32, (LANES, LANES), 1)
    same_head = _group_of(srow, HEAD_DIM) == _group_of(scol, HEAD_DIM)
    log_gamma = [math.log1p(-(2.0 ** (-5.0 - hd))) for hd in range(N_GROUP_HEADS)]

    for pair in range(N_GROUP_HEADS // 2):
        sl = slice(pair * LANES, (pair + 1) * LANES)
        qp, kp, vp = q_ref[0, :, sl], k_ref[0, :, sl], v_ref[0, :, sl]
        lg = jnp.where(low_half, log_gamma[2 * pair], log_gamma[2 * pair + 1])
        halves = []
        for hh in range(2):
            keep = low_half if hh == 0 else jnp.logical_not(low_half)
            qm = jnp.where(keep, qp, jnp.zeros_like(qp))
            decay = jnp.where(visible, jnp.exp(log_gamma[2 * pair + hh] * dist), 0.0)
            s = _dot_nt(qm, kp) * decay
            halves.append(_dot(s.astype(BF16), vp))
        y = jnp.where(low_half, halves[0], halves[1])
        state = state_ref[pair]
        q_dec = (qp.astype(F32) * jnp.exp(lg * (pos + 1.0))).astype(BF16)
        y = y + _dot(q_dec, state.astype(BF16))
        k_dec = (kp.astype(F32) * jnp.exp(lg * (t - 1.0 - pos))).astype(BF16)
        kv_new = jnp.where(same_head, _dot_tn(k_dec, vp), 0.0)
        state_ref[pair] = jnp.exp(lg[0:1, :] * float(t)) * state + kv_new

        inv = 1.0 / HEAD_DIM
        s_lo = jnp.sum(jnp.where(low_half, y, 0.0), axis=1, keepdims=True)
        s_all = jnp.sum(y, axis=1, keepdims=True)
        mu = jnp.where(low_half, s_lo, s_all - s_lo) * inv
        d = y - mu
        d2 = d * d
        v_lo = jnp.sum(jnp.where(low_half, d2, 0.0), axis=1, keepdims=True)
        v_all = jnp.sum(d2, axis=1, keepdims=True)
        var = jnp.where(low_half, v_lo, v_all - v_lo) * inv
        o_ref[0, :, sl] = (d * lax.rsqrt(var + EPS) * g_ref[:, sl]).astype(o_ref.dtype)


def _ret_call(g, q, k, v):
    b, s, _ = q.shape
    t = min(SEQ_TILE, s)
    spec = pl.BlockSpec((1, t, GROUP_WIDTH), lambda bi, ti: (bi, ti, 0))
    return pl.pallas_call(
        _ret_kernel,
        out_shape=jax.ShapeDtypeStruct((b, s, GROUP_WIDTH), BF16),
        grid=(b, s // t),
        in_specs=[pl.BlockSpec(g.shape, lambda bi, ti: (0, 0)), spec, spec, spec],
        out_specs=spec,
        scratch_shapes=[pltpu.VMEM((N_GROUP_HEADS // 2, LANES, LANES), F32)],
        compiler_params=pltpu.CompilerParams(
            dimension_semantics=("parallel", "arbitrary"), vmem_limit_bytes=VMEM_LIMIT),
        name="retention",
    )(g, q, k, v)


def _diff_kernel(lam_ref, g_ref, q_ref, k_ref, vt_ref, o_ref, m_ref, l_ref, alpha_ref, smax_ref,
                 acc_ref, s_scr, p_scr, *, lam_init):
    t = q_ref.shape[1]
    i = pl.program_id(1)
    krow = lax.broadcasted_iota(I32, (t, t), 0)
    qcol = lax.broadcasted_iota(I32, (t, t), 1)
    visible = _group_of(krow, CHUNK) <= _group_of(qcol, CHUNK)
    exp_scale = DIFF_HALF ** -0.5 * LOG2E

    lp = lam_ref[...]
    lam = (jnp.exp(jnp.sum(lp[0:1] * lp[1:2], axis=1, keepdims=True))
           - jnp.exp(jnp.sum(lp[2:3] * lp[3:4], axis=1, keepdims=True)) + lam_init)

    q_parts = []
    for hd in range(N_GROUP_HEADS):
        blk = q_ref[0, :, (hd // 2) * LANES:(hd // 2 + 1) * LANES]
        for which in range(2):
            q_parts.append(_masked_lanes(blk, (hd % 2) * HEAD_DIM + which * DIFF_HALF, DIFF_HALF))

    m_ref[...] = jnp.full(m_ref.shape, NEG_BIG, F32)
    l_ref[...] = jnp.zeros(l_ref.shape, F32)
    acc_ref[...] = jnp.zeros(acc_ref.shape, F32)

    def span(j0, ntile, diag=False):
        rows = ntile * t
        start = pl.multiple_of(j0 * t, t)
        pair_sl = [slice(pr * LANES, (pr + 1) * LANES) for pr in range(N_GROUP_HEADS // 2)]

        def scores(n):
            s = _dot_nt(k_ref[0, pl.ds(start, rows), pair_sl[n // 4]], q_parts[n])
            if diag:
                s = jnp.where(visible, s, NEG_BIG)
            s_scr[n, :rows] = s
            smax_ref[n] = jnp.max(s, axis=0, keepdims=True)

        def softmax(n):
            m_old = m_ref[n]
            m_new = jnp.maximum(m_old, smax_ref[n])
            p = jnp.exp2((s_scr[n, :rows] - m_new) * exp_scale)
            alpha = jnp.exp2((m_old - m_new) * exp_scale)
            l_ref[n] = alpha * l_ref[n] + jnp.sum(p, axis=0, keepdims=True)
            p_scr[n, :rows] = p.astype(BF16)
            alpha_ref[n] = alpha
            m_ref[n] = m_new

        def weighted_values(n):
            head_sl = slice((n // 2) * HEAD_DIM, (n // 2 + 1) * HEAD_DIM)
            pv = _dot(vt_ref[0, j0, head_sl, :], p_scr[n, :t])
            if ntile == 2:
                pv = pv + _dot(vt_ref[0, j0 + 1, head_sl, :], p_scr[n, t:])
            acc_ref[n] = alpha_ref[n] * acc_ref[n] + pv

        _skewed(2 * N_GROUP_HEADS, [(scores, 0), (softmax, 2), (weighted_values, 3)])

    span(i, 1, diag=True)
    _tile_spans(i, span)

    heads = []
    for hd in range(N_GROUP_HEADS):
        y = (acc_ref[2 * hd] / l_ref[2 * hd]
             - lam * (acc_ref[2 * hd + 1] / l_ref[2 * hd + 1]))
        ms = jnp.mean(y * y, axis=0, keepdims=True)
        heads.append(y * lax.rsqrt(ms + EPS))
    out = jnp.concatenate(heads, axis=0).T * g_ref[...] * (1.0 - lam_init)
    o_ref[0] = out.astype(o_ref.dtype)


def _diff_call(lam_params, g, q, k, vt, lam_init):
    b, s, _ = q.shape
    t = min(SEQ_TILE, s)
    q_spec = pl.BlockSpec((1, t, GROUP_WIDTH), lambda bi, qi: (bi, qi, 0))
    k_spec = pl.BlockSpec((1, s, GROUP_WIDTH), lambda bi, qi: (bi, 0, 0))
    vt_spec = pl.BlockSpec((1, s // t, GROUP_WIDTH, t), lambda bi, qi: (bi, 0, 0, 0))
    small = lambda a: pl.BlockSpec(a.shape, lambda bi, qi: (0, 0))
    return pl.pallas_call(
        functools.partial(_diff_kernel, lam_init=lam_init),
        out_shape=jax.ShapeDtypeStruct((b, s, GROUP_WIDTH), BF16),
        grid=(b, s // t),
        in_specs=[small(lam_params), small(g), q_spec, k_spec, vt_spec],
        out_specs=q_spec,
        scratch_shapes=[pltpu.VMEM((2 * N_GROUP_HEADS, 1, t), F32)] * 4
        + [pltpu.VMEM((2 * N_GROUP_HEADS, HEAD_DIM, t), F32),
           pltpu.VMEM((2 * N_GROUP_HEADS, 2 * t, t), F32),
           pltpu.VMEM((2 * N_GROUP_HEADS, 2 * t, t), BF16)],
        compiler_params=pltpu.CompilerParams(
            dimension_semantics=("parallel", "arbitrary"), vmem_limit_bytes=VMEM_LIMIT),
        name="diff_attn",
    )(lam_params, g, q, k, vt)


def _sb_kernel(q_ref, k_ref, vt_ref, o_ref, rest_ref, acc_ref, z_scr, l_scr, suf_scr, a_scr):
    t = q_ref.shape[1]
    i = pl.program_id(1)
    krow = lax.broadcasted_iota(I32, (t, t), 0)
    qcol = lax.broadcasted_iota(I32, (t, t), 1)
    strict = krow < qcol
    suffix_ones = jnp.where(qcol >= krow, -1.0, 0.0).astype(BF16)

    q_heads = [_masked_lanes(q_ref[0, :, (hd // 2) * LANES:(hd // 2 + 1) * LANES],
                             (hd % 2) * HEAD_DIM, HEAD_DIM) for hd in range(N_GROUP_HEADS)]

    rest_ref[...] = jnp.zeros(rest_ref.shape, F32)
    acc_ref[...] = jnp.zeros(acc_ref.shape, F32)

    def group(tiles, diag_first=False):
        def split(c):
            j, hd = tiles[c // N_GROUP_HEADS], c % N_GROUP_HEADS
            return c, j, hd, slice((hd // 2) * LANES, (hd // 2 + 1) * LANES)

        def on_diagonal(c):
            return diag_first and c < N_GROUP_HEADS

        def logits(c):
            c, j, hd, sl = split(c)
            z_scr[c] = _dot_nt(k_ref[0, pl.ds(pl.multiple_of(j * t, t), t), sl], q_heads[hd])

        def log_complement(c):
            z = z_scr[c]
            softplus = jnp.where(z > SOFTPLUS_LINEAR, z, jnp.log(1.0 + jnp.exp(z)))
            if on_diagonal(c):
                softplus = jnp.where(strict, softplus, 0.0)
            l_scr[c] = softplus.astype(BF16)

        def suffix_sums(c):
            suf_scr[c] = _dot(suffix_ones, l_scr[c])

        def weights(c):
            c, j, hd, sl = split(c)
            suffix = suf_scr[c]
            rest = rest_ref[hd]
            a = jnp.exp(z_scr[c] + suffix + rest)
            if on_diagonal(c):
                a = jnp.where(strict, a, 0.0)
            a_scr[c] = a.astype(BF16)
            rest_ref[hd] = rest + suffix[0:1, :]

        def weighted_values(c):
            c, j, hd, sl = split(c)
            head_sl = slice(hd * HEAD_DIM, (hd + 1) * HEAD_DIM)
            acc_ref[hd] = acc_ref[hd] + _dot(vt_ref[0, j, head_sl, :], a_scr[c])

        _skewed(len(tiles) * N_GROUP_HEADS,
                [(logits, 0), (log_complement, 2), (suffix_sums, 3), (weights, 5),
                 (weighted_values, 6)])

    @pl.when(i == 0)
    def _():
        group([i], diag_first=True)

    @pl.when(i > 0)
    def _():
        group([i, i - 1], diag_first=True)

    def more_tiles(carry):
        n_done, live = carry
        return jnp.logical_and(n_done < i, live > SB_DEAD_LOG)

    def next_tile(carry):
        n_done, _ = carry
        group([i - 1 - n_done])
        return n_done + 1, jnp.max(rest_ref[...])

    lax.while_loop(more_tiles, next_tile, (jnp.int32(1), jnp.max(rest_ref[...])))

    o_ref[0] = acc_ref[...].reshape(GROUP_WIDTH, t).T.astype(o_ref.dtype)


def _sb_call(q, k, vt):
    b, s, _ = q.shape
    t = min(SEQ_TILE, s)
    q_spec = pl.BlockSpec((1, t, GROUP_WIDTH), lambda bi, qi: (bi, qi, 0))
    k_spec = pl.BlockSpec((1, s, GROUP_WIDTH), lambda bi, qi: (bi, 0, 0))
    vt_spec = pl.BlockSpec((1, s // t, GROUP_WIDTH, t), lambda bi, qi: (bi, 0, 0, 0))
    return pl.pallas_call(
        _sb_kernel,
        out_shape=jax.ShapeDtypeStruct((b, s, GROUP_WIDTH), BF16),
        grid=(b, s // t),
        in_specs=[q_spec, k_spec, vt_spec],
        out_specs=q_spec,
        scratch_shapes=[pltpu.VMEM((N_GROUP_HEADS, 1, t), F32),
                        pltpu.VMEM((N_GROUP_HEADS, HEAD_DIM, t), F32),
                        pltpu.VMEM((2 * N_GROUP_HEADS, t, t), F32),
                        pltpu.VMEM((2 * N_GROUP_HEADS, t, t), BF16),
                        pltpu.VMEM((2 * N_GROUP_HEADS, t, t), F32),
                        pltpu.VMEM((2 * N_GROUP_HEADS, t, t), BF16)],
        compiler_params=pltpu.CompilerParams(
            dimension_semantics=("parallel", "arbitrary"), vmem_limit_bytes=VMEM_LIMIT),
        name="stick_breaking",
    )(q, k, vt)


def _out_kernel(x_ref, ya_ref, yb_ref, yc_ref, yd_ref, gate_ref, w_ref, o_ref, y_scr):
    for g, y_ref in enumerate((ya_ref, yb_ref, yc_ref, yd_ref)):
        sl = slice(g * GROUP_WIDTH, (g + 1) * GROUP_WIDTH)
        gate = gate_ref[:, sl].astype(F32)
        silu = gate / (1.0 + jnp.exp(-gate))
        y_scr[:, sl] = (y_ref[...].astype(F32) * silu).astype(BF16)
    o_ref[...] = x_ref[...] + _dot(y_scr[...], w_ref[...])


def _out_call(xf, ya, yb, yc, yd, gates, w):
    rows, d_model = xf.shape
    tm = min(ROW_TILE, rows)
    row_spec = lambda width: pl.BlockSpec((tm, width), lambda i: (i, 0))
    return pl.pallas_call(
        _out_kernel,
        out_shape=jax.ShapeDtypeStruct((rows, d_model), F32),
        grid=(rows // tm,),
        in_specs=[row_spec(d_model)] + [row_spec(GROUP_WIDTH)] * 4
        + [row_spec(4 * GROUP_WIDTH), pl.BlockSpec(w.shape, lambda i: (0, 0))],
        out_specs=row_spec(d_model),
        scratch_shapes=[pltpu.VMEM((tm, 4 * GROUP_WIDTH), BF16)],
        compiler_params=pltpu.CompilerParams(
            dimension_semantics=("parallel",), vmem_limit_bytes=VMEM_LIMIT),
        name="out_proj",
    )(xf, ya, yb, yc, yd, gates, w)


def _pack_w_in(w):
    d = w.shape[0]
    off = np.cumsum([0, GROUP_WIDTH, KV_RANK, IDX_HEADS * IDX_DIM, IDX_DIM, IDX_HEADS, GROUP_WIDTH]
                    + [GROUP_WIDTH] * 12)
    seg = lambda n: w[:, off[n]:off[n + 1]]
    zeros = lambda width: jnp.zeros((d, width), w.dtype)
    qa = seg(0)
    parts = []
    for hd in range(N_GROUP_HEADS):
        parts += [qa[:, hd * HEAD_DIM:(hd + 1) * HEAD_DIM], zeros(LANES - HEAD_DIM)]
    parts += [seg(1), seg(2), seg(3), seg(3)]
    parts += [seg(6), seg(7), seg(8)]
    parts += [seg(10), seg(11)]
    parts += [seg(14), seg(15)]
    parts += [seg(5), seg(9), seg(13), seg(17)]
    packed = jnp.concatenate(parts, axis=1).astype(BF16)
    packed_t = jnp.concatenate([seg(4), zeros(LANES - IDX_HEADS), seg(12), seg(16)],
                               axis=1).T.astype(BF16)
    return packed, packed_t


def _rope_tables(positions, theta, rot_dim, period):
    half = rot_dim // 2
    inv = theta ** (-jnp.arange(half, dtype=F32) * 2.0 / rot_dim)
    ang = positions.astype(F32)[..., None] * inv
    cos, sin = jnp.cos(ang), jnp.sin(ang)
    pad = period - rot_dim
    shape = ang.shape[:-1] + (pad,)
    cos_p = jnp.concatenate([cos, cos, jnp.ones(shape, F32)], axis=-1)
    sin_p = jnp.concatenate([-sin, sin, jnp.zeros(shape, F32)], axis=-1)
    reps = LANES // period
    cos_t = jnp.tile(cos_p, (1, 1, reps)).reshape(-1, LANES)
    sin_t = jnp.tile(sin_p, (1, 1, reps)).reshape(-1, LANES)
    return cos_t, sin_t


def _pad_lanes(v, fill):
    return jnp.concatenate([v.astype(F32), jnp.full((LANES - v.shape[0],), fill, F32)])


def kernel(x, positions, norm_g, w_in, kv_norm_g, w_kv_up, q_norm_a, k_norm_a, ret_norm_g,
           q_norm_c, k_norm_c, lam_q1, lam_k1, lam_q2, lam_k2, subln_g, w_out):
    b, s, d_model = x.shape
    depth = norm_g.shape[0]
    topk = min(TOPK_MAX, s // 4)
    assert s % SEQ_TILE == 0 and (b * s) % ROW_TILE == 0
    nt = s // SEQ_TILE

    tables = (_rope_tables(positions, ROPE_THETA, HEAD_DIM // 4, HEAD_DIM)
              + _rope_tables(positions, RET_THETA, HEAD_DIM, HEAD_DIM)
              + _rope_tables(positions, ROPE_THETA, DIFF_HALF // 4, DIFF_HALF))

    xf = x.reshape(b * s, d_model)
    for l in range(depth):
        gains = jnp.stack([
            _pad_lanes(q_norm_a[l], 1.0),
            kv_norm_g[l].astype(F32),
            _pad_lanes(k_norm_a[l], 1.0),
            jnp.tile(q_norm_c[l].astype(F32), LANES // DIFF_HALF),
            jnp.tile(k_norm_c[l].astype(F32), LANES // DIFF_HALF),
            jnp.zeros((LANES,), F32), jnp.zeros((LANES,), F32), jnp.zeros((LANES,), F32)])
        w_packed, w_packed_t = _pack_w_in(w_in[l])
        (qa, kv, kvt, qidx, kidx, widxt, qb, kb, vb, qc, kc, vct, qd, kd, vdt, gates) = _proj_call(
            xf, norm_g[l][None, :].astype(F32), w_packed, w_packed_t, w_kv_up[l].astype(BF16),
            gains, tables)

        seq = lambda a: a.reshape(b, s, a.shape[-1])
        slabs = lambda a: a.reshape(b, nt, a.shape[-2], a.shape[-1])
        y_a = _dsa_call(seq(qa), seq(qidx), slabs(widxt), seq(kidx), seq(kv), slabs(kvt), topk)
        y_b = _ret_call(ret_norm_g[l][None, :].astype(F32), seq(qb), seq(kb), seq(vb))
        lam_init = 0.8 - 0.6 * math.exp(-0.3 * l)
        lam_params = jnp.stack(
            [_pad_lanes(v[l], 0.0) for v in (lam_q1, lam_k1, lam_q2, lam_k2)]
            + [jnp.zeros((LANES,), F32)] * 4)
        subln = jnp.tile(subln_g[l].astype(F32), N_GROUP_HEADS)[None, :]
        y_c = _diff_call(lam_params, subln, seq(qc), seq(kc), slabs(vct), lam_init)
        y_d = _sb_call(seq(qd), seq(kd), slabs(vdt))

        flat = lambda a: a.reshape(b * s, a.shape[-1])
        xf = _out_call(xf, flat(y_a), flat(y_b), flat(y_c), flat(y_d), gates,
                       w_out[l].astype(BF16))
    return xf.reshape(b, s, d_model)
```
